```python
import math
import jax, jax.numpy as jnp
from jax import lax
import numpy as np

D_MODEL = 1024
BATCH = 2
SEQ = 8192
DEPTH = 1

MIX_WIDTH = D_MODEL
GLA_HEADS = 4
GLA_DK = D_MODEL // 16
GLA_DV = D_MODEL // 8
GLA_KEY_WIDTH = GLA_HEADS * GLA_DK
GLA_WIDTH = GLA_HEADS * GLA_DV
GLA_GATE_RANK = 16
GLA_GATE_TEMP = 16.0
GLA_CHUNK = 64
SB_HEADS = 8
SB_DH = 64
SB_WIDTH = SB_HEADS * SB_DH
SB_BLOCK = 128
PEER_HEADS = 8
PEER_NKEYS = 128
PEER_NEXPERTS = PEER_NKEYS * PEER_NKEYS
PEER_QDIM = 256
PEER_HALF = PEER_QDIM // 2
PEER_TOPK = 16
PEER_TOKEN_BLOCK = 128
ALPHA = (2.0 * DEPTH) ** 0.25
BETA = (8.0 * DEPTH) ** -0.25
EPS = 1e-5

SPLIT_NAMES = ('q_gla', 'k_gla', 'v_gla', 'g_gla', 'a_gla', 'q_sb', 'k_sb', 'v_sb')
SPLIT_SIZES = (GLA_KEY_WIDTH, GLA_KEY_WIDTH, GLA_WIDTH, GLA_WIDTH, GLA_GATE_RANK,
               SB_WIDTH, SB_WIDTH, SB_WIDTH)
IN_WIDTH = sum(SPLIT_SIZES)
SPLIT_POINTS = [int(v) for v in np.cumsum(SPLIT_SIZES)[:-1]]

kernel_name = 'hybrid_gla_stickbreak_peer_deepnorm'


def layer_norm(x, g, b):
    xf = x.astype(jnp.float32)
    mu = jnp.mean(xf, axis=-1, keepdims=True)
    var = jnp.mean(jnp.square(xf - mu), axis=-1, keepdims=True)
    return ((xf - mu) * lax.rsqrt(var + EPS) * g + b).astype(x.dtype)


def rms_norm(x, g):
    xf = x.astype(jnp.float32)
    return (xf * lax.rsqrt(jnp.mean(xf * xf, axis=-1, keepdims=True) + EPS) * g).astype(x.dtype)


def gla_mixer(q, k, v, log_a):
    B, S = q.shape[0], q.shape[1]
    nc = S // GLA_CHUNK
    f32 = jnp.float32

    def to_chunks(t):
        return t.reshape(B, nc, GLA_CHUNK, t.shape[2], t.shape[3]).transpose(1, 0, 3, 2, 4).astype(f32)

    q, k, v, log_a = to_chunks(q) * (GLA_DK ** -0.5), to_chunks(k), to_chunks(v), to_chunks(log_a)
    b = jnp.cumsum(log_a, axis=-2)
    b_last = b[..., -1:, :]
    q_in = q * jnp.exp(b)
    k_in = k * jnp.exp(-b)
    k_state = k * jnp.exp(b_last - b)
    decay = jnp.exp(b_last)[..., 0, :, None]

    causal = jnp.tril(jnp.ones((GLA_CHUNK, GLA_CHUNK), dtype=bool))
    att = jnp.where(causal, jnp.einsum('nbhcd,nbhjd->nbhcj', q_in, k_in), 0.0)
    o_intra = jnp.einsum('nbhcj,nbhje->nbhce', att, v)

    def step(state, inp):
        qi, ks, vi, dec = inp
        o_inter = jnp.einsum('bhcd,bhde->bhce', qi, state)
        state = state * dec + jnp.einsum('bhjd,bhje->bhde', ks, vi)
        return state, o_inter

    state0 = jnp.zeros((B, q.shape[2], GLA_DK, GLA_DV), f32)
    _, o_inter = lax.scan(step, state0, (q_in, k_state, v, decay))
    o = o_intra + o_inter
    return o.transpose(1, 0, 3, 2, 4).reshape(B, S, GLA_HEADS, GLA_DV)


def stick_breaking(q, k, v):
    B, S, H, D = q.shape
    nb = S // SB_BLOCK
    scale = D ** -0.5
    qb = q.reshape(B, nb, SB_BLOCK, H, D).transpose(1, 0, 3, 2, 4)
    kT = k.transpose(0, 2, 1, 3)
    vT = v.transpose(0, 2, 1, 3)
    key_pos = jnp.arange(S)

    def block(args):
        qi, blk = args
        q_pos = blk * SB_BLOCK + jnp.arange(SB_BLOCK)
        z = jnp.einsum('bhqd,bhsd->bhqs', qi, kT).astype(jnp.float32) * scale
        mask = key_pos[None, :] < q_pos[:, None]
        log_not_break = jnp.where(mask, jax.nn.log_sigmoid(-z), 0.0)
        tail = lax.cumsum(log_not_break, axis=3, reverse=True) - log_not_break
        w = jnp.where(mask, jnp.exp(jax.nn.log_sigmoid(z) + tail), 0.0)
        return jnp.einsum('bhqs,bhsd->bhqd', w.astype(vT.dtype), vT)

    o = lax.map(block, (qb, jnp.arange(nb)))
    return o.transpose(1, 0, 3, 2, 4).reshape(B, S, H, D)


def hybrid_mixer(x, w_in, w_gla_gate, b_gla_gate, gla_norm_g, sb_norm_g, w_out):
    B, S, _ = x.shape
    proj = x @ w_in
    q_g, k_g, v_g, g_g, a_lr, q_s, k_s, v_s = jnp.split(proj, SPLIT_POINTS, axis=-1)

    log_a = jax.nn.log_sigmoid((a_lr @ w_gla_gate + b_gla_gate).astype(jnp.float32)) / GLA_GATE_TEMP
    o_gla = gla_mixer(q_g.reshape(B, S, GLA_HEADS, GLA_DK),
                      k_g.reshape(B, S, GLA_HEADS, GLA_DK),
                      v_g.reshape(B, S, GLA_HEADS, GLA_DV),
                      log_a.reshape(B, S, GLA_HEADS, GLA_DK))
    o_gla = rms_norm(o_gla, gla_norm_g).reshape(B, S, GLA_WIDTH).astype(x.dtype) * jax.nn.silu(g_g)

    o_sb = stick_breaking(q_s.reshape(B, S, SB_HEADS, SB_DH),
                          k_s.reshape(B, S, SB_HEADS, SB_DH),
                          v_s.reshape(B, S, SB_HEADS, SB_DH))
    o_sb = rms_norm(o_sb, sb_norm_g).reshape(B, S, SB_WIDTH).astype(x.dtype)

    return jnp.concatenate([o_gla, o_sb], axis=-1) @ w_out


def peer_ffn(x, w_query, sub_keys, expert_u, expert_v):
    B, S, D = x.shape
    T = B * S
    xt = x.reshape(T, D)
    qry = (xt @ w_query).reshape(T, PEER_HEADS, 2, PEER_HALF)
    scores = jnp.einsum('thpc,hpnc->thpn', qry, sub_keys).astype(jnp.float32)
    s_top, i_top = lax.top_k(scores, PEER_TOPK)
    cand = (s_top[:, :, 0, :, None] + s_top[:, :, 1, None, :]).reshape(T, PEER_HEADS, PEER_TOPK * PEER_TOPK)
    cand_idx = (i_top[:, :, 0, :, None] * PEER_NKEYS + i_top[:, :, 1, None, :]).reshape(T, PEER_HEADS, PEER_TOPK * PEER_TOPK)
    best, pos = lax.top_k(cand, PEER_TOPK)
    idx = jnp.take_along_axis(cand_idx, pos, axis=-1)
    gate = jax.nn.softmax(best, axis=-1)
    HK = PEER_HEADS * PEER_TOPK
    nb = T // PEER_TOKEN_BLOCK

    def block(args):
        xb, ib, gb = args
        u = expert_u[ib]
        h = jax.nn.gelu(jnp.einsum('tkd,td->tk', u, xb).astype(jnp.float32), approximate=False)
        vsel = expert_v[ib]
        return jnp.einsum('tk,tkd->td', (gb * h).astype(xb.dtype), vsel)

    out = lax.map(block, (xt.reshape(nb, PEER_TOKEN_BLOCK, D),
                          idx.reshape(nb, PEER_TOKEN_BLOCK, HK),
                          gate.reshape(nb, PEER_TOKEN_BLOCK, HK)))
    return out.reshape(B, S, D)


def setup_inputs(seed: int = 0) -> dict:
    key = jax.random.key(seed)
    ks = jax.random.split(key, 16)
    L, D = DEPTH, D_MODEL
    f32 = jnp.float32

    def nrm(k, shape, scale):
        return jax.random.normal(k, shape, f32) * scale

    col_scale = np.concatenate([np.full((s,), BETA if n in ('v_gla', 'v_sb') else 1.0, np.float32)
                                for n, s in zip(SPLIT_NAMES, SPLIT_SIZES)])
    return {
        'x': nrm(ks[0], (BATCH, SEQ, D), 1.0),
        'w_in': nrm(ks[1], (L, D, IN_WIDTH), D ** -0.5) * jnp.asarray(col_scale),
        'w_gla_gate': nrm(ks[2], (L, GLA_GATE_RANK, GLA_KEY_WIDTH), GLA_GATE_RANK ** -0.5),
        'b_gla_gate': nrm(ks[3], (L, GLA_KEY_WIDTH), 0.1),
        'gla_norm_g': 1.0 + nrm(ks[4], (L, GLA_HEADS, GLA_DV), 0.02),
        'sb_norm_g': 1.0 + nrm(ks[5], (L, SB_HEADS, SB_DH), 0.02),
        'w_out': nrm(ks[6], (L, MIX_WIDTH, D), MIX_WIDTH ** -0.5 * BETA),
        'ln1_g': 1.0 + nrm(ks[7], (L, D), 0.02),
        'ln1_b': nrm(ks[8], (L, D), 0.02),
        'peer_w_query': nrm(ks[9], (L, D, PEER_HEADS * PEER_QDIM), D ** -0.5),
        'peer_sub_keys': nrm(ks[10], (L, PEER_HEADS, 2, PEER_NKEYS, PEER_HALF), PEER_HALF ** -0.5),
        'peer_u': nrm(ks[11], (L, PEER_NEXPERTS, D), D ** -0.5 * BETA),
        'peer_v': nrm(ks[12], (L, PEER_NEXPERTS, D), BETA),
        'ln2_g': 1.0 + nrm(ks[13], (L, D), 0.02),
        'ln2_b': nrm(ks[14], (L, D), 0.02),
    }


def reference(x, w_in, w_gla_gate, b_gla_gate, gla_norm_g, sb_norm_g, w_out, ln1_g, ln1_b,
              peer_w_query, peer_sub_keys, peer_u, peer_v, ln2_g, ln2_b):
    for l in range(DEPTH):
        mix = hybrid_mixer(x, w_in[l], w_gla_gate[l], b_gla_gate[l], gla_norm_g[l], sb_norm_g[l], w_out[l])
        x = layer_norm(ALPHA * x + mix, ln1_g[l], ln1_b[l])
        ffn = peer_ffn(x, peer_w_query[l], peer_sub_keys[l], peer_u[l], peer_v[l])
        x = layer_norm(ALPHA * x + ffn, ln2_g[l], ln2_b[l])
    return x
```

```python
import functools

import jax
import jax.numpy as jnp
from jax import lax
from jax.experimental import pallas as pl
from jax.experimental.pallas import tpu as pltpu

F32 = jnp.float32
BF16 = jnp.bfloat16

LANES = 128
GLA_HEADS = 4
GLA_DK = 64
GLA_DV = 128
GLA_RANK = 16
GLA_GATE_TEMP = 16.0
GLA_CHUNK = 64
SB_HEADS = 8
SB_DH = 64
PEER_HEADS = 8
PEER_NKEYS = 128
PEER_HALF = 128
PEER_TOPK = 16
DEPTH = 1
ALPHA = (2.0 * DEPTH) ** 0.25
EPS = 1e-5
NEG_INF = float("-inf")
SB_DEAD = -104.0

VMEM_LIMIT = 56 * 1024 * 1024


def _dot(a, b):
    return jnp.dot(a, b, preferred_element_type=F32)


def _dot_nt(a, b):
    return lax.dot_general(a, b, (((1,), (1,)), ((), ())), preferred_element_type=F32)


def _dot_tn(a, b):
    return lax.dot_general(a, b, (((0,), (0,)), ((), ())), preferred_element_type=F32)


def _split_bf16(v):
    hi = v.astype(BF16)
    lo = (v - hi.astype(F32)).astype(BF16)
    return hi, lo


def _log_sigmoid(z):
    return jnp.minimum(z, 0.0) - jnp.log(1.0 + jnp.exp(-jnp.abs(z)))


IN_SPLITS = (("qg", 256, F32), ("kg", 256, F32), ("vg", 512, BF16), ("gg", 512, F32),
             ("alr", LANES, F32), ("qs", 512, BF16), ("ks", 512, BF16), ("vs", 512, BF16))


def _inproj_kernel(x_ref, w_ref, *out_refs):
    xb = x_ref[...].astype(BF16)
    off = 0
    for (name, width, _), o_ref in zip(IN_SPLITS, out_refs):
        r = _dot(xb, w_ref[:, off:off + width])
        if name in ("qg", "qs"):
            r = r * (GLA_DK ** -0.5)
        o_ref[...] = r.astype(o_ref.dtype)
        off += width


def _inproj(x2d, w_cat, tm):
    T, D = x2d.shape
    wtot = w_cat.shape[1]
    return pl.pallas_call(
        _inproj_kernel,
        grid=(T // tm,),
        in_specs=[pl.BlockSpec((tm, D), lambda i: (i, 0)),
                  pl.BlockSpec((D, wtot), lambda i: (0, 0))],
        out_specs=[pl.BlockSpec((tm, w), lambda i: (i, 0)) for _, w, _ in IN_SPLITS],
        out_shape=[jax.ShapeDtypeStruct((T, w), dt) for _, w, dt in IN_SPLITS],
        compiler_params=pltpu.CompilerParams(dimension_semantics=("arbitrary",),
                                             vmem_limit_bytes=VMEM_LIMIT),
        name="inproj",
    )(x2d, w_cat)


def _gla_kernel(q_ref, k_ref, v_ref, g_ref, a_ref, wg_ref, bg_ref, gn_ref, o_ref, st_ref, *, n_chunks):
    @pl.when(pl.program_id(2) == 0)
    def _():
        st_ref[...] = jnp.zeros_like(st_ref)

    C = GLA_CHUNK
    lane = lax.broadcasted_iota(jnp.int32, (1, 2 * GLA_DK), 1)
    head_mask = [lane < GLA_DK, lane >= GLA_DK]
    r_i = lax.broadcasted_iota(jnp.int32, (C, C), 0)
    c_i = lax.broadcasted_iota(jnp.int32, (C, C), 1)
    causal = c_i <= r_i
    tri = jnp.where(causal, 1.0, 0.0).astype(BF16)

    la_all = _log_sigmoid(_dot(a_ref[...].astype(BF16), wg_ref[...]) + bg_ref[...]) * (1.0 / GLA_GATE_TEMP)

    for c in range(n_chunks):
        rows = slice(c * C, (c + 1) * C)
        la = la_all[rows]
        hi, lo = _split_bf16(la)
        b = _dot(tri, hi) + _dot(tri, lo)
        b_last = b[C - 1:C, :]
        q = q_ref[rows, :]
        k = k_ref[rows, :]
        q_in = q * jnp.exp(b)
        k_in = (k * jnp.exp(-b)).astype(BF16)
        k_st = (k * jnp.exp(b_last - b)).astype(BF16)
        dec = jnp.exp(b_last)
        v = v_ref[rows, :]
        st = st_ref[...]
        st_b = st.astype(BF16)
        outs = []
        for h in range(2):
            qm = jnp.where(head_mask[h], q_in, 0.0).astype(BF16)
            att = jnp.where(causal, _dot_nt(qm, k_in), 0.0)
            vh = v[:, h * GLA_DV:(h + 1) * GLA_DV]
            o_h = _dot(att.astype(BF16), vh) + _dot_nt(qm, st_b[h * GLA_DV:(h + 1) * GLA_DV, :])
            ms = jnp.mean(o_h * o_h, axis=-1, keepdims=True)
            outs.append(o_h * lax.rsqrt(ms + EPS))
        st_ref[...] = st * dec + _dot_tn(v, k_st)
        o = jnp.concatenate(outs, axis=1) * gn_ref[...]
        g = g_ref[rows, :]
        o_ref[rows, :] = (o * (g * jax.nn.sigmoid(g))).astype(o_ref.dtype)


def _gla(qg, kg, vg, gg, alr, wg, bg, gn, B, S, ls):
    n_s = S // ls
    row = lambda b, hp, s: (b * n_s + s, hp)
    row0 = lambda b, hp, s: (b * n_s + s, 0)
    par = lambda b, hp, s: (0, hp)
    return pl.pallas_call(
        functools.partial(_gla_kernel, n_chunks=ls // GLA_CHUNK),
        grid=(B, GLA_HEADS // 2, n_s),
        in_specs=[pl.BlockSpec((ls, 2 * GLA_DK), row),
                  pl.BlockSpec((ls, 2 * GLA_DK), row),
                  pl.BlockSpec((ls, 2 * GLA_DV), row),
                  pl.BlockSpec((ls, 2 * GLA_DV), row),
                  pl.BlockSpec((ls, LANES), row0),
                  pl.BlockSpec((LANES, 2 * GLA_DK), par),
                  pl.BlockSpec((1, 2 * GLA_DK), par),
                  pl.BlockSpec((1, 2 * GLA_DV), par)],
        out_specs=pl.BlockSpec((ls, 2 * GLA_DV), row),
        out_shape=jax.ShapeDtypeStruct((B * S, GLA_HEADS * GLA_DV), BF16),
        scratch_shapes=[pltpu.VMEM((2 * GLA_DV, 2 * GLA_DK), F32)],
        compiler_params=pltpu.CompilerParams(dimension_semantics=("arbitrary", "arbitrary", "arbitrary"),
                                             vmem_limit_bytes=VMEM_LIMIT),
        name="gla",
    )(qg, kg, vg, gg, alr, wg, bg, gn)


SB_T = 128


def _sb_kernel(q_ref, k_ref, v_ref, tu_ref, gn_ref, o_ref, acc_ref, car_ref):
    i = pl.program_id(2)
    lane = lax.broadcasted_iota(jnp.int32, (1, 2 * SB_DH), 1)
    head_mask = [lane < SB_DH, lane >= SB_DH]
    q2 = q_ref[...]
    qh = [jnp.where(head_mask[h], q2, jnp.zeros_like(q2)) for h in range(2)]
    row = lax.broadcasted_iota(jnp.int32, (SB_T, SB_T), 0)
    col = lax.broadcasted_iota(jnp.int32, (SB_T, SB_T), 1)
    acc_ref[...] = jnp.zeros_like(acc_ref)
    car_ref[...] = jnp.zeros_like(car_ref)

    def cond(carry):
        j, alive = carry
        return jnp.logical_and(j >= 0, alive > SB_DEAD)

    def body(carry):
        j, _ = carry
        ks = pl.multiple_of(j * SB_T, SB_T)
        k = k_ref[pl.ds(ks, SB_T), :]
        v = v_ref[pl.ds(ks, SB_T), :]
        valid = (col + (j - i) * SB_T) < row
        alive = None
        for h in range(2):
            z = _dot_nt(qh[h], k)
            lnb = jnp.where(valid, -(jnp.maximum(z, 0.0) + jnp.log(1.0 + jnp.exp(-jnp.abs(z)))), 0.0)
            hi, lo = _split_bf16(lnb)
            cs = _dot(jnp.concatenate([hi, lo], axis=1), tu_ref[...])
            car = car_ref[h]
            w = jnp.where(valid, jnp.exp(z + lnb + cs[:, :SB_T] + car), 0.0)
            acc_ref[h] += _dot(w.astype(BF16), v)
            car = car + cs[:, SB_T:]
            car_ref[h] = car
            m = jnp.max(car)
            alive = m if alive is None else jnp.maximum(alive, m)
        return j - 1, alive

    lax.while_loop(cond, body, (i, jnp.float32(0.0)))

    o = jnp.where(head_mask[0], acc_ref[0], acc_ref[1])
    sq = o * o
    ms0 = jnp.sum(jnp.where(head_mask[0], sq, 0.0), axis=-1, keepdims=True)
    ms1 = jnp.sum(jnp.where(head_mask[1], sq, 0.0), axis=-1, keepdims=True)
    ms = jnp.where(head_mask[0], ms0, ms1) * (1.0 / SB_DH)
    o_ref[...] = (o * lax.rsqrt(ms + EPS) * gn_ref[...]).astype(o_ref.dtype)


def _sb(qs, ks, vs, tu, gn, B, S):
    n_q = S // SB_T
    return pl.pallas_call(
        _sb_kernel,
        grid=(B, SB_HEADS // 2, n_q),
        in_specs=[pl.BlockSpec((SB_T, 2 * SB_DH), lambda b, hp, i: (b * n_q + i, hp)),
                  pl.BlockSpec((S, 2 * SB_DH), lambda b, hp, i: (b, hp)),
                  pl.BlockSpec((S, 2 * SB_DH), lambda b, hp, i: (b, hp)),
                  pl.BlockSpec((2 * SB_T, 2 * SB_T), lambda b, hp, i: (0, 0)),
                  pl.BlockSpec((1, 2 * SB_DH), lambda b, hp, i: (0, hp))],
        out_specs=pl.BlockSpec((SB_T, 2 * SB_DH), lambda b, hp, i: (b * n_q + i, hp)),
        out_shape=jax.ShapeDtypeStruct((B * S, SB_HEADS * SB_DH), BF16),
        scratch_shapes=[pltpu.VMEM((2, SB_T, 2 * SB_DH), F32),
                        pltpu.VMEM((2, SB_T, SB_T), F32)],
        compiler_params=pltpu.CompilerParams(dimension_semantics=("arbitrary", "arbitrary", "arbitrary"),
                                             vmem_limit_bytes=VMEM_LIMIT),
        name="sb",
    )(qs, ks, vs, tu, gn)


def _layer_norm(y, g, b):
    mu = jnp.mean(y, axis=-1, keepdims=True)
    d = y - mu
    var = jnp.mean(d * d, axis=-1, keepdims=True)
    return d * lax.rsqrt(var + EPS) * g + b


def _outproj_kernel(og_ref, os_ref, x_ref, wo_ref, g_ref, b_ref, x1_ref, x1b_ref):
    half = og_ref.shape[1]
    mix = _dot(og_ref[...], wo_ref[:half, :]) + _dot(os_ref[...], wo_ref[half:, :])
    x1 = _layer_norm(ALPHA * x_ref[...] + mix, g_ref[...], b_ref[...])
    x1_ref[...] = x1
    x1b_ref[...] = x1.astype(BF16)


def _outproj(ogla, osb, x2d, wo, g, b, tm):
    T, D = x2d.shape
    half = ogla.shape[1]
    return pl.pallas_call(
        _outproj_kernel,
        grid=(T // tm,),
        in_specs=[pl.BlockSpec((tm, half), lambda i: (i, 0)),
                  pl.BlockSpec((tm, half), lambda i: (i, 0)),
                  pl.BlockSpec((tm, D), lambda i: (i, 0)),
                  pl.BlockSpec((2 * half, D), lambda i: (0, 0)),
                  pl.BlockSpec((1, D), lambda i: (0, 0)),
                  pl.BlockSpec((1, D), lambda i: (0, 0))],
        out_specs=[pl.BlockSpec((tm, D), lambda i: (i, 0)),
                   pl.BlockSpec((tm, D), lambda i: (i, 0))],
        out_shape=[jax.ShapeDtypeStruct((T, D), F32), jax.ShapeDtypeStruct((T, D), BF16)],
        compiler_params=pltpu.CompilerParams(dimension_semantics=("arbitrary",),
                                             vmem_limit_bytes=VMEM_LIMIT),
        name="outproj",
    )(ogla, osb, x2d, wo, g, b)


def _top_rows(s, n):
    tops = []
    for _ in range(n):
        m = jnp.max(s, axis=0, keepdims=True)
        tops.append(m)
        s = jnp.where(s == m, NEG_INF, s)
    return tops


def _peer_sel_kernel(x_ref, wq_ref, sk_ref, s1_ref, e1_ref, th_ref, c_ref, q_scr):
    tt = x_ref.shape[0]
    qry = _dot(x_ref[...], wq_ref[...]).astype(BF16)
    for hp in range(2 * PEER_HEADS):
        q_scr[hp] = qry[:, hp * PEER_HALF:(hp + 1) * PEER_HALF]
    sub = lax.broadcasted_iota(jnp.int32, (8, tt), 0)

    def head(h, _):
        s0 = _dot_nt(sk_ref[2 * h], q_scr[2 * h])
        s1 = _dot_nt(sk_ref[2 * h + 1], q_scr[2 * h + 1])
        n = PEER_TOPK + 1
        a = _top_rows(s0, n)
        b = _top_rows(s1, n)
        pad = [jnp.full_like(b[0], NEG_INF)] * (24 - n)
        b24 = jnp.concatenate(b + pad, axis=0)
        b8 = b24[:8]
        cand = [a[0] + b24]
        for k in range(2, n + 1):
            cand.append(jnp.where(sub < n // k, a[k - 1] + b8, NEG_INF))
        best = _top_rows(jnp.concatenate(cand, axis=0), n)
        z = jnp.zeros_like(best[0])
        for r in range(PEER_TOPK):
            z = z + jnp.exp(best[r] - best[0])
        tau = 0.5 * (best[PEER_TOPK - 1] + best[PEER_TOPK])
        s1_ref[h] = s1
        e1_ref[h] = jnp.exp(s1 - b[0])
        th_ref[h] = tau - s0
        c_ref[h] = jnp.exp(s0 - a[0]) / z
        return 0

    lax.fori_loop(0, PEER_HEADS, head, 0)


def _peer_sel(x1b, wq, sk, tt):
    T, D = x1b.shape
    qd = wq.shape[1]
    shp = jax.ShapeDtypeStruct((PEER_HEADS, PEER_NKEYS, T), F32)
    ospec = pl.BlockSpec((PEER_HEADS, PEER_NKEYS, tt), lambda i: (0, 0, i))
    return pl.pallas_call(
        _peer_sel_kernel,
        grid=(T // tt,),
        in_specs=[pl.BlockSpec((tt, D), lambda i: (i, 0)),
                  pl.BlockSpec((D, qd), lambda i: (0, 0)),
                  pl.BlockSpec((2 * PEER_HEADS, PEER_NKEYS, PEER_HALF), lambda i: (0, 0, 0))],
        out_specs=[ospec, ospec, ospec, ospec],
        out_shape=[shp, shp, shp, shp],
        scratch_shapes=[pltpu.VMEM((2 * PEER_HEADS, tt, PEER_HALF), BF16)],
        compiler_params=pltpu.CompilerParams(dimension_semantics=("arbitrary",),
                                             vmem_limit_bytes=VMEM_LIMIT),
        name="peer_sel",
    )(x1b, wq, sk)


PEER_ET = 1024
PEER_LC = 256


def _peer_ffn_kernel(xb_ref, x1_ref, u_ref, vt_ref, s1_ref, e1_ref, th_ref, c_ref, g_ref, b_ref,
                     o_ref, acc_ref, h_ref, p_ref):
    e = pl.program_id(1)
    tt = xb_ref.shape[0]
    n_i = PEER_ET // PEER_NKEYS

    @pl.when(e == 0)
    def _():
        acc_ref[...] = jnp.zeros_like(acc_ref)

    h_ref[...] = _dot_nt(u_ref[...], xb_ref[...])

    def per_i(ii, _):
        r0 = pl.multiple_of(ii * PEER_NKEYS, PEER_NKEYS)
        for lc in range(tt // PEER_LC):
            ls = slice(lc * PEER_LC, (lc + 1) * PEER_LC)
            gate = jnp.zeros((PEER_NKEYS, PEER_LC), F32)
            for h in range(PEER_HEADS):
                th = th_ref[h, pl.ds(ii, 1), ls]
                cc = c_ref[h, pl.ds(ii, 1), ls]
                gate = gate + jnp.where(s1_ref[h, :, ls] >= th, e1_ref[h, :, ls] * cc, 0.0)
            hh = h_ref[pl.ds(r0, PEER_NKEYS), ls]
            act = 0.5 * hh * (1.0 + lax.erf(hh * (2.0 ** -0.5)))
            p_ref[pl.ds(r0, PEER_NKEYS), ls] = (gate * act).astype(BF16)
        return 0

    lax.fori_loop(0, n_i, per_i, 0)
    acc_ref[...] += _dot(vt_ref[...], p_ref[...])

    @pl.when(e == pl.num_programs(1) - 1)
    def _():
        y = ALPHA * x1_ref[...] + acc_ref[...].T
        o_ref[...] = _layer_norm(y, g_ref[...], b_ref[...])


def _peer_ffn(x1b, x1, u, vt, s1, e1, th, cc, g, b, tt):
    T, D = x1.shape
    E = u.shape[0]
    n_i = PEER_ET // PEER_NKEYS
    tok = pl.BlockSpec((PEER_HEADS, PEER_NKEYS, tt), lambda t, e: (0, 0, t))
    sel = pl.BlockSpec((PEER_HEADS, n_i, tt), lambda t, e: (0, e, t))
    return pl.pallas_call(
        _peer_ffn_kernel,
        grid=(T // tt, E // PEER_ET),
        in_specs=[pl.BlockSpec((tt, D), lambda t, e: (t, 0)),
                  pl.BlockSpec((tt, D), lambda t, e: (t, 0)),
                  pl.BlockSpec((PEER_ET, D), lambda t, e: (e, 0)),
                  pl.BlockSpec((D, PEER_ET), lambda t, e: (0, e)),
                  tok, tok, sel, sel,
                  pl.BlockSpec((1, D), lambda t, e: (0, 0)),
                  pl.BlockSpec((1, D), lambda t, e: (0, 0))],
        out_specs=pl.BlockSpec((tt, D), lambda t, e: (t, 0)),
        out_shape=jax.ShapeDtypeStruct((T, D), F32),
        scratch_shapes=[pltpu.VMEM((D, tt), F32),
                        pltpu.VMEM((PEER_ET, tt), F32),
                        pltpu.VMEM((PEER_ET, tt), BF16)],
        compiler_params=pltpu.CompilerParams(dimension_semantics=("arbitrary", "arbitrary"),
                                             vmem_limit_bytes=VMEM_LIMIT),
        name="peer_ffn",
    )(x1b, x1, u, vt, s1, e1, th, cc, g, b)


def _layer(x, w_in, w_gla_gate, b_gla_gate, gla_norm_g, sb_norm_g, w_out, ln1_g, ln1_b,
           peer_w_query, peer_sub_keys, peer_u, peer_v, ln2_g, ln2_b):
    B, S, D = x.shape
    T = B * S
    x2d = x.reshape(T, D)

    kw, gw, sw = GLA_HEADS * GLA_DK, GLA_HEADS * GLA_DV, SB_HEADS * SB_DH
    pts = [0, kw, 2 * kw, 2 * kw + gw, 2 * kw + 2 * gw, 2 * kw + 2 * gw + GLA_RANK]
    pts += [pts[-1] + sw, pts[-1] + 2 * sw, pts[-1] + 3 * sw]
    cols = [w_in[:, pts[n]:pts[n + 1]] for n in range(8)]
    cols[4] = jnp.pad(cols[4], ((0, 0), (0, LANES - GLA_RANK)))
    w_cat = jnp.concatenate(cols, axis=1).astype(BF16)
    wg = jnp.pad(w_gla_gate, ((0, LANES - GLA_RANK), (0, 0))).astype(BF16)

    qg, kg, vg, gg, alr, qs, ks, vs = _inproj(x2d, w_cat, min(512, T))

    ogla = _gla(qg, kg, vg, gg, alr, wg, b_gla_gate.reshape(1, -1), gla_norm_g.reshape(1, -1),
                B, S, min(512, S))

    r = lax.broadcasted_iota(jnp.int32, (2 * SB_T, 2 * SB_T), 0) % SB_T
    c = lax.broadcasted_iota(jnp.int32, (2 * SB_T, 2 * SB_T), 1)
    tu = jnp.where((c >= SB_T) | (r > c), 1.0, 0.0).astype(BF16)
    osb = _sb(qs, ks, vs, tu, sb_norm_g.reshape(1, -1), B, S)

    x1, x1b = _outproj(ogla, osb, x2d, w_out.astype(BF16), ln1_g.reshape(1, D), ln1_b.reshape(1, D),
                       min(512, T))

    sk = peer_sub_keys.reshape(2 * PEER_HEADS, PEER_NKEYS, PEER_HALF).astype(BF16)
    s1, e1, th, cc = _peer_sel(x1b, peer_w_query.astype(BF16), sk, min(256, T))

    out = _peer_ffn(x1b, x1, peer_u.astype(BF16), peer_v.T.astype(BF16), s1, e1, th, cc,
                    ln2_g.reshape(1, D), ln2_b.reshape(1, D), min(512, T))
    return out.reshape(B, S, D)


def kernel(x, w_in, w_gla_gate, b_gla_gate, gla_norm_g, sb_norm_g, w_out, ln1_g, ln1_b,
           peer_w_query, peer_sub_keys, peer_u, peer_v, ln2_g, ln2_b):
    for l in range(DEPTH):
        x = _layer(x, w_in[l], w_gla_gate[l], b_gla_gate[l], gla_norm_g[l], sb_norm_g[l], w_out[l],
                   ln1_g[l], ln1_b[l], peer_w_query[l], peer_sub_keys[l], peer_u[l], peer_v[l],
                   ln2_g[l], ln2_b[l])
    return x
```

```python
import functools

import jax
import jax.numpy as jnp
from jax import lax
from jax.experimental import pallas as pl
from jax.experimental.pallas import tpu as pltpu

F32 = jnp.float32
BF16 = jnp.bfloat16

LANES = 128
GLA_HEADS = 4
GLA_DK = 64
GLA_DV = 128
GLA_RANK = 16
GLA_GATE_TEMP = 16.0
GLA_CHUNK = 64
SB_HEADS = 8
SB_DH = 64
PEER_HEADS = 8
PEER_NKEYS = 128
PEER_HALF = 128
PEER_TOPK = 16
DEPTH = 1
ALPHA = (2.0 * DEPTH) ** 0.25
EPS = 1e-5
NEG_INF = float("-inf")
SB_DEAD = -104.0

VMEM_LIMIT = 56 * 1024 * 1024


def _dot(a, b):
    return jnp.dot(a, b, preferred_element_type=F32)


def _dot_nt(a, b):
    return lax.dot_general(a, b, (((1,), (1,)), ((), ())), preferred_element_type=F32)


def _dot_tn(a, b):
    return lax.dot_general(a, b, (((0,), (0,)), ((), ())), preferred_element_type=F32)


def _split_bf16(v):
    hi = v.astype(BF16)
    lo = (v - hi.astype(F32)).astype(BF16)
    return hi, lo


def _log_sigmoid(z):
    return jnp.minimum(z, 0.0) - jnp.log(1.0 + jnp.exp(-jnp.abs(z)))


IN_SPLITS = (("qg", 256, F32), ("kg", 256, F32), ("vg", 512, BF16), ("gg", 512, F32),
             ("alr", LANES, F32), ("qs", 512, BF16), ("ks", 512, BF16), ("vs", 512, BF16))


def _inproj_kernel(x_ref, w_ref, *out_refs):
    xb = x_ref[...].astype(BF16)
    off = 0
    for (name, width, _), o_ref in zip(IN_SPLITS, out_refs):
        r = _dot(xb, w_ref[:, off:off + width])
        if name in ("qg", "qs"):
            r = r * (GLA_DK ** -0.5)
        o_ref[...] = r.astype(o_ref.dtype)
        off += width


def _inproj(x2d, w_cat, tm):
    T, D = x2d.shape
    wtot = w_cat.shape[1]
    return pl.pallas_call(
        _inproj_kernel,
        grid=(T // tm,),
        in_specs=[pl.BlockSpec((tm, D), lambda i: (i, 0)),
                  pl.BlockSpec((D, wtot), lambda i: (0, 0))],
        out_specs=[pl.BlockSpec((tm, w), lambda i: (i, 0)) for _, w, _ in IN_SPLITS],
        out_shape=[jax.ShapeDtypeStruct((T, w), dt) for _, w, dt in IN_SPLITS],
        compiler_params=pltpu.CompilerParams(dimension_semantics=("arbitrary",),
                                             vmem_limit_bytes=VMEM_LIMIT),
        name="inproj",
    )(x2d, w_cat)


def _gla_kernel(q_ref, k_ref, v_ref, g_ref, a_ref, wg_ref, bg_ref, gn_ref, o_ref, st_ref, *, n_chunks):
    @pl.when(pl.program_id(2) == 0)
    def _():
        st_ref[...] = jnp.zeros_like(st_ref)

    C = GLA_CHUNK
    lane = lax.broadcasted_iota(jnp.int32, (1, 2 * GLA_DK), 1)
    head_mask = [lane < GLA_DK, lane >= GLA_DK]
    r_i = lax.broadcasted_iota(jnp.int32, (C, C), 0)
    c_i = lax.broadcasted_iota(jnp.int32, (C, C), 1)
    causal = c_i <= r_i
    tri = jnp.where(causal, 1.0, 0.0).astype(BF16)

    la_all = _log_sigmoid(_dot(a_ref[...].astype(BF16), wg_ref[...]) + bg_ref[...]) * (1.0 / GLA_GATE_TEMP)

    for c in range(n_chunks):
        rows = slice(c * C, (c + 1) * C)
        la = la_all[rows]
        hi, lo = _split_bf16(la)
        b = _dot(tri, hi) + _dot(tri, lo)
        b_last = b[C - 1:C, :]
        q = q_ref[rows, :]
        k = k_ref[rows, :]
        q_in = q * jnp.exp(b)
        k_in = (k * jnp.exp(-b)).astype(BF16)
        k_st = (k * jnp.exp(b_last - b)).astype(BF16)
        dec = jnp.exp(b_last)
        v = v_ref[rows, :]
        st = st_ref[...]
        st_b = st.astype(BF16)
        outs = []
        for h in range(2):
            qm = jnp.where(head_mask[h], q_in, 0.0).astype(BF16)
            att = jnp.where(causal, _dot_nt(qm, k_in), 0.0)
            vh = v[:, h * GLA_DV:(h + 1) * GLA_DV]
            o_h = _dot(att.astype(BF16), vh) + _dot_nt(qm, st_b[h * GLA_DV:(h + 1) * GLA_DV, :])
            ms = jnp.mean(o_h * o_h, axis=-1, keepdims=True)
            outs.append(o_h * lax.rsqrt(ms + EPS))
        st_ref[...] = st * dec + _dot_tn(v, k_st)
        o = jnp.concatenate(outs, axis=1) * gn_ref[...]
        g = g_ref[rows, :]
        o_ref[rows, :] = (o * (g * jax.nn.sigmoid(g))).astype(o_ref.dtype)


def _gla(qg, kg, vg, gg, alr, wg, bg, gn, B, S, ls):
    n_s = S // ls
    row = lambda b, hp, s: (b * n_s + s, hp)
    row0 = lambda b, hp, s: (b * n_s + s, 0)
    par = lambda b, hp, s: (0, hp)
    return pl.pallas_call(
        functools.partial(_gla_kernel, n_chunks=ls // GLA_CHUNK),
        grid=(B, GLA_HEADS // 2, n_s),
        in_specs=[pl.BlockSpec((ls, 2 * GLA_DK), row),
                  pl.BlockSpec((ls, 2 * GLA_DK), row),
                  pl.BlockSpec((ls, 2 * GLA_DV), row),
                  pl.BlockSpec((ls, 2 * GLA_DV), row),
                  pl.BlockSpec((ls, LANES), row0),
                  pl.BlockSpec((LANES, 2 * GLA_DK), par),
                  pl.BlockSpec((1, 2 * GLA_DK), par),
                  pl.BlockSpec((1, 2 * GLA_DV), par)],
        out_specs=pl.BlockSpec((ls, 2 * GLA_DV), row),
        out_shape=jax.ShapeDtypeStruct((B * S, GLA_HEADS * GLA_DV), BF16),
        scratch_shapes=[pltpu.VMEM((2 * GLA_DV, 2 * GLA_DK), F32)],
        compiler_params=pltpu.CompilerParams(dimension_semantics=("arbitrary", "arbitrary", "arbitrary"),
                                             vmem_limit_bytes=VMEM_LIMIT),
        name="gla",
    )(qg, kg, vg, gg, alr, wg, bg, gn)


SB_T = 128
SB_GW = 2 * LANES
SB_GROUPS = SB_HEADS * SB_DH // SB_GW
SB_TILES = 4 * SB_GROUPS


def _sb_kernel(q_ref, k_ref, v_ref, tu_ref, gn_ref, o_ref,
               acc_ref, car_ref, kb_ref, vb_ref, zl_ref, l_ref, cs_ref):
    i = pl.program_id(1)
    T = SB_T
    lane = lax.broadcasted_iota(jnp.int32, (1, SB_GW), 1)
    is_h0 = (lane % LANES) < SB_DH
    qg = []
    for g in range(SB_GROUPS):
        q = q_ref[:, g * SB_GW:(g + 1) * SB_GW]
        zero = jnp.zeros_like(q)
        qg.append(jnp.concatenate([jnp.where(is_h0, q, zero), jnp.where(is_h0, zero, q)], axis=0))
    row = lax.broadcasted_iota(jnp.int32, (T, T), 0)
    col = lax.broadcasted_iota(jnp.int32, (T, T), 1)
    acc_ref[...] = jnp.zeros_like(acc_ref)
    car_ref[...] = jnp.zeros_like(car_ref)
    kb_ref[...] = jnp.zeros_like(kb_ref)
    vb_ref[...] = jnp.zeros_like(vb_ref)

    def cond(carry):
        j, alive = carry
        return jnp.logical_and(j >= 0, alive > SB_DEAD)

    def body(carry):
        j, _ = carry
        ks = pl.multiple_of(j * T, T)
        valid = (col + (j - i) * T) < row
        for g in range(SB_GROUPS):
            for p in range(2):
                src = slice(g * SB_GW + p * LANES, g * SB_GW + (p + 1) * LANES)
                kb_ref[g, p * T:(p + 1) * T, p * LANES:(p + 1) * LANES] = k_ref[pl.ds(ks, T), src]
                vb_ref[g, p * T:(p + 1) * T, p * LANES:(p + 1) * LANES] = v_ref[pl.ds(ks, T), src]
            z = _dot_nt(qg[g], kb_ref[g])
            for h in range(2):
                for p in range(2):
                    n = 4 * g + 2 * h + p
                    zq = z[h * T:(h + 1) * T, p * T:(p + 1) * T]
                    lnb = jnp.where(valid, -(jnp.maximum(zq, 0.0) + jnp.log(1.0 + jnp.exp(-jnp.abs(zq)))), 0.0)
                    hi, lo = _split_bf16(lnb)
                    l_ref[n * T:(n + 1) * T, :T] = hi
                    l_ref[n * T:(n + 1) * T, T:] = lo
                    zl_ref[n] = zq + lnb
        cs_ref[...] = _dot(l_ref[...], tu_ref[...])
        top = None
        for g in range(SB_GROUPS):
            w_rows = []
            for h in range(2):
                w_cols = []
                for p in range(2):
                    n = 4 * g + 2 * h + p
                    car = car_ref[n]
                    w = jnp.where(valid, jnp.exp(zl_ref[n] + cs_ref[n * T:(n + 1) * T, :T] + car), 0.0)
                    w_cols.append(w.astype(BF16))
                    car = car + cs_ref[n * T:(n + 1) * T, T:]
                    car_ref[n] = car
                    top = car if top is None else jnp.maximum(top, car)
                w_rows.append(jnp.concatenate(w_cols, axis=1))
            acc_ref[g] += _dot(jnp.concatenate(w_rows, axis=0), vb_ref[g])
        return j - 1, jnp.max(top)

    lax.while_loop(cond, body, (i, jnp.float32(0.0)))

    head_of_lane = lane // SB_DH
    for g in range(SB_GROUPS):
        a = acc_ref[g]
        o = jnp.where(is_h0, a[:T], a[T:])
        sq = o * o
        ms = jnp.zeros_like(o)
        for hd in range(SB_GW // SB_DH):
            mine = head_of_lane == hd
            ms = jnp.where(mine, jnp.sum(jnp.where(mine, sq, 0.0), axis=-1, keepdims=True), ms)
        cols = slice(g * SB_GW, (g + 1) * SB_GW)
        o_ref[:, cols] = (o * lax.rsqrt(ms * (1.0 / SB_DH) + EPS) * gn_ref[:, cols]).astype(o_ref.dtype)


def _sb(qs, ks, vs, tu, gn, B, S):
    n_q = S // SB_T
    width = SB_HEADS * SB_DH
    return pl.pallas_call(
        _sb_kernel,
        grid=(B, n_q),
        in_specs=[pl.BlockSpec((SB_T, width), lambda b, i: (b * n_q + i, 0)),
                  pl.BlockSpec((S, width), lambda b, i: (b, 0)),
                  pl.BlockSpec((S, width), lambda b, i: (b, 0)),
                  pl.BlockSpec((2 * SB_T, 2 * SB_T), lambda b, i: (0, 0)),
                  pl.BlockSpec((1, width), lambda b, i: (0, 0))],
        out_specs=pl.BlockSpec((SB_T, width), lambda b, i: (b * n_q + i, 0)),
        out_shape=jax.ShapeDtypeStruct((B * S, width), BF16),
        scratch_shapes=[pltpu.VMEM((SB_GROUPS, 2 * SB_T, SB_GW), F32),
                        pltpu.VMEM((SB_TILES, SB_T, SB_T), F32),
                        pltpu.VMEM((SB_GROUPS, 2 * SB_T, SB_GW), BF16),
                        pltpu.VMEM((SB_GROUPS, 2 * SB_T, SB_GW), BF16),
                        pltpu.VMEM((SB_TILES, SB_T, SB_T), F32),
                        pltpu.VMEM((SB_TILES * SB_T, 2 * SB_T), BF16),
                        pltpu.VMEM((SB_TILES * SB_T, 2 * SB_T), F32)],
        compiler_params=pltpu.CompilerParams(dimension_semantics=("arbitrary", "arbitrary"),
                                             vmem_limit_bytes=VMEM_LIMIT),
        name="sb",
    )(qs, ks, vs, tu, gn)


def _layer_norm(y, g, b):
    mu = jnp.mean(y, axis=-1, keepdims=True)
    d = y - mu
    var = jnp.mean(d * d, axis=-1, keepdims=True)
    return d * lax.rsqrt(var + EPS) * g + b


def _outproj_kernel(og_ref, os_ref, x_ref, wo_ref, g_ref, b_ref, x1_ref, x1b_ref):
    half = og_ref.shape[1]
    mix = _dot(og_ref[...], wo_ref[:half, :]) + _dot(os_ref[...], wo_ref[half:, :])
    x1 = _layer_norm(ALPHA * x_ref[...] + mix, g_ref[...], b_ref[...])
    x1_ref[...] = x1
    x1b_ref[...] = x1.astype(BF16)


def _outproj(ogla, osb, x2d, wo, g, b, tm):
    T, D = x2d.shape
    half = ogla.shape[1]
    return pl.pallas_call(
        _outproj_kernel,
        grid=(T // tm,),
        in_specs=[pl.BlockSpec((tm, half), lambda i: (i, 0)),
                  pl.BlockSpec((tm, half), lambda i: (i, 0)),
                  pl.BlockSpec((tm, D), lambda i: (i, 0)),
                  pl.BlockSpec((2 * half, D), lambda i: (0, 0)),
                  pl.BlockSpec((1, D), lambda i: (0, 0)),
                  pl.BlockSpec((1, D), lambda i: (0, 0))],
        out_specs=[pl.BlockSpec((tm, D), lambda i: (i, 0)),
                   pl.BlockSpec((tm, D), lambda i: (i, 0))],
        out_shape=[jax.ShapeDtypeStruct((T, D), F32), jax.ShapeDtypeStruct((T, D), BF16)],
        compiler_params=pltpu.CompilerParams(dimension_semantics=("arbitrary",),
                                             vmem_limit_bytes=VMEM_LIMIT),
        name="outproj",
    )(ogla, osb, x2d, wo, g, b)


def _top_rows(s, n):
    tops = []
    for _ in range(n):
        m = jnp.max(s, axis=0, keepdims=True)
        tops.append(m)
        s = jnp.where(s == m, NEG_INF, s)
    return tops


def _peer_sel_kernel(x_ref, wq_ref, sk_ref, s1_ref, e1_ref, th_ref, c_ref, q_scr):
    tt = x_ref.shape[0]
    qry = _dot(x_ref[...], wq_ref[...]).astype(BF16)
    for hp in range(2 * PEER_HEADS):
        q_scr[hp] = qry[:, hp * PEER_HALF:(hp + 1) * PEER_HALF]
    sub = lax.broadcasted_iota(jnp.int32, (8, tt), 0)

    def head(h, _):
        s0 = _dot_nt(sk_ref[2 * h], q_scr[2 * h])
        s1 = _dot_nt(sk_ref[2 * h + 1], q_scr[2 * h + 1])
        n = PEER_TOPK + 1
        a = _top_rows(s0, n)
        b = _top_rows(s1, n)
        pad = [jnp.full_like(b[0], NEG_INF)] * (24 - n)
        b24 = jnp.concatenate(b + pad, axis=0)
        b8 = b24[:8]
        cand = [a[0] + b24]
        for k in range(2, n + 1):
            cand.append(jnp.where(sub < n // k, a[k - 1] + b8, NEG_INF))
        best = _top_rows(jnp.concatenate(cand, axis=0), n)
        z = jnp.zeros_like(best[0])
        for r in range(PEER_TOPK):
            z = z + jnp.exp(best[r] - best[0])
        tau = 0.5 * (best[PEER_TOPK - 1] + best[PEER_TOPK])
        s1_ref[0, h] = s1
        e1_ref[0, h] = jnp.exp(s1 - b[0])
        th_ref[0, h] = tau - s0
        c_ref[0, h] = jnp.exp(s0 - a[0]) / z
        return 0

    lax.fori_loop(0, PEER_HEADS, head, 0)


def _peer_sel(x1b, wq, sk, tt):
    T, D = x1b.shape
    qd = wq.shape[1]
    shp = jax.ShapeDtypeStruct((T // tt, PEER_HEADS, PEER_NKEYS, tt), F32)
    ospec = pl.BlockSpec((1, PEER_HEADS, PEER_NKEYS, tt), lambda i: (i, 0, 0, 0))
    return pl.pallas_call(
        _peer_sel_kernel,
        grid=(T // tt,),
        in_specs=[pl.BlockSpec((tt, D), lambda i: (i, 0)),
                  pl.BlockSpec((D, qd), lambda i: (0, 0)),
                  pl.BlockSpec((2 * PEER_HEADS, PEER_NKEYS, PEER_HALF), lambda i: (0, 0, 0))],
        out_specs=[ospec, ospec, ospec, ospec],
        out_shape=[shp, shp, shp, shp],
        scratch_shapes=[pltpu.VMEM((2 * PEER_HEADS, tt, PEER_HALF), BF16)],
        compiler_params=pltpu.CompilerParams(dimension_semantics=("arbitrary",),
                                             vmem_limit_bytes=VMEM_LIMIT),
        name="peer_sel",
    )(x1b, wq, sk)


PEER_ET = 1024
PEER_EQ = 256
PEER_LC = 256


def _peer_ffn_kernel(xt_ref, x1_ref, u_ref, vt_ref, s1_ref, e1_ref, th_ref, c_ref, g_ref, b_ref,
                     o_ref, acc_ref, h_ref, p_ref):
    e = pl.program_id(1)
    n_lc = xt_ref.shape[0]
    n_q = PEER_ET // PEER_EQ
    n_k = PEER_EQ // PEER_NKEYS
    assert n_lc == 2

    @pl.when(e == 0)
    def _():
        acc_ref[...] = jnp.zeros_like(acc_ref)

    def scores(q, lc):
        r0 = q * PEER_EQ if isinstance(q, int) else pl.multiple_of(q * PEER_EQ, PEER_EQ)
        h_ref[lc] = _dot(u_ref[pl.ds(r0, PEER_EQ), :], xt_ref[lc])

    def gated_act(q, lc):
        for k in range(n_k):
            rows = slice(k * PEER_NKEYS, (k + 1) * PEER_NKEYS)
            gate = jnp.zeros((PEER_NKEYS, PEER_LC), F32)
            for h in range(PEER_HEADS):
                th = th_ref[lc, h, pl.ds(q * n_k + k, 1), :]
                cc = c_ref[lc, h, pl.ds(q * n_k + k, 1), :]
                gate = gate + jnp.where(s1_ref[lc, h] >= th, e1_ref[lc, h] * cc, 0.0)
            hh = h_ref[lc, rows, :]
            act = 0.5 * hh * (1.0 + lax.erf(hh * (2.0 ** -0.5)))
            p_ref[lc, rows, :] = (gate * act).astype(BF16)

    def accumulate(q, lc):
        acc_ref[lc] += _dot(vt_ref[q], p_ref[lc])

    scores(0, 0)
    scores(0, 1)
    gated_act(0, 0)

    def step(q, _):
        scores(q + 1, 0)
        gated_act(q, 1)
        accumulate(q, 0)
        scores(q + 1, 1)
        gated_act(q + 1, 0)
        accumulate(q, 1)
        return 0

    lax.fori_loop(0, n_q - 1, step, 0)
    gated_act(n_q - 1, 1)
    accumulate(n_q - 1, 0)
    accumulate(n_q - 1, 1)

    @pl.when(e == pl.num_programs(1) - 1)
    def _():
        for lc in range(n_lc):
            rows = slice(lc * PEER_LC, (lc + 1) * PEER_LC)
            y = ALPHA * x1_ref[rows, :] + acc_ref[lc].T
            o_ref[rows, :] = _layer_norm(y, g_ref[...], b_ref[...])


def _peer_ffn(xt, x1, u, vt, s1, e1, th, cc, g, b, tt):
    T, D = x1.shape
    E = u.shape[0]
    n_lc = tt // PEER_LC
    n_q = PEER_ET // PEER_EQ
    tok = pl.BlockSpec((n_lc, PEER_HEADS, PEER_NKEYS, PEER_LC), lambda t, e: (t, 0, 0, 0))
    sel = pl.BlockSpec((n_lc, PEER_HEADS, PEER_ET // PEER_NKEYS, PEER_LC), lambda t, e: (t, 0, e, 0))
    return pl.pallas_call(
        _peer_ffn_kernel,
        grid=(T // tt, E // PEER_ET),
        in_specs=[pl.BlockSpec((n_lc, D, PEER_LC), lambda t, e: (t, 0, 0)),
                  pl.BlockSpec((tt, D), lambda t, e: (t, 0)),
                  pl.BlockSpec((PEER_ET, D), lambda t, e: (e, 0)),
                  pl.BlockSpec((n_q, D, PEER_EQ), lambda t, e: (e, 0, 0)),
                  tok, tok, sel, sel,
                  pl.BlockSpec((1, D), lambda t, e: (0, 0)),
                  pl.BlockSpec((1, D), lambda t, e: (0, 0))],
        out_specs=pl.BlockSpec((tt, D), lambda t, e: (t, 0)),
        out_shape=jax.ShapeDtypeStruct((T, D), F32),
        scratch_shapes=[pltpu.VMEM((n_lc, D, PEER_LC), F32),
                        pltpu.VMEM((n_lc, PEER_EQ, PEER_LC), F32),
                        pltpu.VMEM((n_lc, PEER_EQ, PEER_LC), BF16)],
        compiler_params=pltpu.CompilerParams(dimension_semantics=("arbitrary", "arbitrary"),
                                             vmem_limit_bytes=VMEM_LIMIT),
        name="peer_ffn",
    )(xt, x1, u, vt, s1, e1, th, cc, g, b)


def _layer(x, w_in, w_gla_gate, b_gla_gate, gla_norm_g, sb_norm_g, w_out, ln1_g, ln1_b,
           peer_w_query, peer_sub_keys, peer_u, peer_v, ln2_g, ln2_b):
    B, S, D = x.shape
    T = B * S
    x2d = x.reshape(T, D)

    kw, gw, sw = GLA_HEADS * GLA_DK, GLA_HEADS * GLA_DV, SB_HEADS * SB_DH
    pts = [0, kw, 2 * kw, 2 * kw + gw, 2 * kw + 2 * gw, 2 * kw + 2 * gw + GLA_RANK]
    pts += [pts[-1] + sw, pts[-1] + 2 * sw, pts[-1] + 3 * sw]
    cols = [w_in[:, pts[n]:pts[n + 1]] for n in range(8)]
    cols[4] = jnp.pad(cols[4], ((0, 0), (0, LANES - GLA_RANK)))
    w_cat = jnp.concatenate(cols, axis=1).astype(BF16)
    wg = jnp.pad(w_gla_gate, ((0, LANES - GLA_RANK), (0, 0))).astype(BF16)

    qg, kg, vg, gg, alr, qs, ks, vs = _inproj(x2d, w_cat, min(512, T))

    ogla = _gla(qg, kg, vg, gg, alr, wg, b_gla_gate.reshape(1, -1), gla_norm_g.reshape(1, -1),
                B, S, min(512, S))

    r = lax.broadcasted_iota(jnp.int32, (2 * SB_T, 2 * SB_T), 0) % SB_T
    c = lax.broadcasted_iota(jnp.int32, (2 * SB_T, 2 * SB_T), 1)
    tu = jnp.where((c >= SB_T) | (r > c), 1.0, 0.0).astype(BF16)
    osb = _sb(qs, ks, vs, tu, sb_norm_g.reshape(1, -1), B, S)

    x1, x1b = _outproj(ogla, osb, x2d, w_out.astype(BF16), ln1_g.reshape(1, D), ln1_b.reshape(1, D),
                       min(512, T))

    sk = peer_sub_keys.reshape(2 * PEER_HEADS, PEER_NKEYS, PEER_HALF).astype(BF16)
    s1, e1, th, cc = _peer_sel(x1b, peer_w_query.astype(BF16), sk, PEER_LC)

    xt = x1b.reshape(-1, PEER_LC, D).transpose(0, 2, 1)
    vt = peer_v.reshape(-1, PEER_EQ, D).transpose(0, 2, 1).astype(BF16)
    out = _peer_ffn(xt, x1, peer_u.astype(BF16), vt, s1, e1, th, cc,
                    ln2_g.reshape(1, D), ln2_b.reshape(1, D), 2 * PEER_LC)
    return out.reshape(B, S, D)


def kernel(x, w_in, w_gla_gate, b_gla_gate, gla_norm_g, sb_norm_g, w_out, ln1_g, ln1_b,
           peer_w_query, peer_sub_keys, peer_u, peer_v, ln2_g, ln2_b):
    for l in range(DEPTH):
        x = _layer(x, w_in[l], w_gla_gate[l], b_gla_gate[l], gla_norm_g[l], sb_norm_g[l], w_out[l],
                   ln1_g[l], ln1_b[l], peer_w_query[l], peer_sub_keys[l], peer_u[l], peer_v[l],
                   ln2_g[l], ln2_b[l])
    return x
```

```python
import functools

import jax
import jax.numpy as jnp
from jax import lax
from jax.experimental import pallas as pl
from jax.experimental.pallas import tpu as pltpu

F32 = jnp.float32
BF16 = jnp.bfloat16

LANES = 128
GLA_HEADS = 4
GLA_DK = 64
GLA_DV = 128
GLA_RANK = 16
GLA_GATE_TEMP = 16.0
GLA_CHUNK = 64
SB_HEADS = 8
SB_DH = 64
PEER_HEADS = 8
PEER_NKEYS = 128
PEER_HALF = 128
PEER_TOPK = 16
DEPTH = 1
ALPHA = (2.0 * DEPTH) ** 0.25
EPS = 1e-5
NEG_INF = float("-inf")
SB_DEAD = -104.0

VMEM_LIMIT = 56 * 1024 * 1024


def _dot(a, b):
    return jnp.dot(a, b, preferred_element_type=F32)


def _dot_nt(a, b):
    return lax.dot_general(a, b, (((1,), (1,)), ((), ())), preferred_element_type=F32)


def _dot_tn(a, b):
    return lax.dot_general(a, b, (((0,), (0,)), ((), ())), preferred_element_type=F32)


def _split_bf16(v):
    hi = v.astype(BF16)
    lo = (v - hi.astype(F32)).astype(BF16)
    return hi, lo


def _log_sigmoid(z):
    return jnp.minimum(z, 0.0) - jnp.log(1.0 + jnp.exp(-jnp.abs(z)))


IN_SPLITS = (("qg", 256, F32), ("kg", 256, F32), ("vg", 512, BF16), ("gg", 512, F32),
             ("alr", LANES, F32), ("qs", 512, BF16), ("ks", 512, BF16), ("vs", 512, BF16))


def _inproj_kernel(x_ref, w_ref, *out_refs):
    xb = x_ref[...].astype(BF16)
    off = 0
    for (name, width, _), o_ref in zip(IN_SPLITS, out_refs):
        r = _dot(xb, w_ref[:, off:off + width])
        if name in ("qg", "qs"):
            r = r * (GLA_DK ** -0.5)
        o_ref[...] = r.astype(o_ref.dtype)
        off += width


def _inproj(x2d, w_cat, tm):
    T, D = x2d.shape
    wtot = w_cat.shape[1]
    return pl.pallas_call(
        _inproj_kernel,
        grid=(T // tm,),
        in_specs=[pl.BlockSpec((tm, D), lambda i: (i, 0)),
                  pl.BlockSpec((D, wtot), lambda i: (0, 0))],
        out_specs=[pl.BlockSpec((tm, w), lambda i: (i, 0)) for _, w, _ in IN_SPLITS],
        out_shape=[jax.ShapeDtypeStruct((T, w), dt) for _, w, dt in IN_SPLITS],
        compiler_params=pltpu.CompilerParams(dimension_semantics=("arbitrary",),
                                             vmem_limit_bytes=VMEM_LIMIT),
        name="inproj",
    )(x2d, w_cat)


def _gla_kernel(q_ref, k_ref, v_ref, g_ref, a_ref, wg_ref, bg_ref, gn_ref, o_ref, st_ref, *, n_chunks):
    @pl.when(pl.program_id(2) == 0)
    def _():
        st_ref[...] = jnp.zeros_like(st_ref)

    C = GLA_CHUNK
    lane = lax.broadcasted_iota(jnp.int32, (1, 2 * GLA_DK), 1)
    head_mask = [lane < GLA_DK, lane >= GLA_DK]
    r_i = lax.broadcasted_iota(jnp.int32, (C, C), 0)
    c_i = lax.broadcasted_iota(jnp.int32, (C, C), 1)
    causal = c_i <= r_i
    tri = jnp.where(causal, 1.0, 0.0).astype(BF16)

    la_all = _log_sigmoid(_dot(a_ref[...].astype(BF16), wg_ref[...]) + bg_ref[...]) * (1.0 / GLA_GATE_TEMP)

    for c in range(n_chunks):
        rows = slice(c * C, (c + 1) * C)
        la = la_all[rows]
        hi, lo = _split_bf16(la)
        b = _dot(tri, hi) + _dot(tri, lo)
        b_last = b[C - 1:C, :]
        q = q_ref[rows, :]
        k = k_ref[rows, :]
        q_in = q * jnp.exp(b)
        k_in = (k * jnp.exp(-b)).astype(BF16)
        k_st = (k * jnp.exp(b_last - b)).astype(BF16)
        dec = jnp.exp(b_last)
        v = v_ref[rows, :]
        st = st_ref[...]
        st_b = st.astype(BF16)
        outs = []
        for h in range(2):
            qm = jnp.where(head_mask[h], q_in, 0.0).astype(BF16)
            att = jnp.where(causal, _dot_nt(qm, k_in), 0.0)
            vh = v[:, h * GLA_DV:(h + 1) * GLA_DV]
            o_h = _dot(att.astype(BF16), vh) + _dot_nt(qm, st_b[h * GLA_DV:(h + 1) * GLA_DV, :])
            ms = jnp.mean(o_h * o_h, axis=-1, keepdims=True)
            outs.append(o_h * lax.rsqrt(ms + EPS))
        st_ref[...] = st * dec + _dot_tn(v, k_st)
        o = jnp.concatenate(outs, axis=1) * gn_ref[...]
        g = g_ref[rows, :]
        o_ref[rows, :] = (o * (g * jax.nn.sigmoid(g))).astype(o_ref.dtype)


def _gla(qg, kg, vg, gg, alr, wg, bg, gn, B, S, ls):
    n_s = S // ls
    row = lambda b, hp, s: (b * n_s + s, hp)
    row0 = lambda b, hp, s: (b * n_s + s, 0)
    par = lambda b, hp, s: (0, hp)
    return pl.pallas_call(
        functools.partial(_gla_kernel, n_chunks=ls // GLA_CHUNK),
        grid=(B, GLA_HEADS // 2, n_s),
        in_specs=[pl.BlockSpec((ls, 2 * GLA_DK), row),
                  pl.BlockSpec((ls, 2 * GLA_DK), row),
                  pl.BlockSpec((ls, 2 * GLA_DV), row),
                  pl.BlockSpec((ls, 2 * GLA_DV), row),
                  pl.BlockSpec((ls, LANES), row0),
                  pl.BlockSpec((LANES, 2 * GLA_DK), par),
                  pl.BlockSpec((1, 2 * GLA_DK), par),
                  pl.BlockSpec((1, 2 * GLA_DV), par)],
        out_specs=pl.BlockSpec((ls, 2 * GLA_DV), row),
        out_shape=jax.ShapeDtypeStruct((B * S, GLA_HEADS * GLA_DV), BF16),
        scratch_shapes=[pltpu.VMEM((2 * GLA_DV, 2 * GLA_DK), F32)],
        compiler_params=pltpu.CompilerParams(dimension_semantics=("arbitrary", "arbitrary", "arbitrary"),
                                             vmem_limit_bytes=VMEM_LIMIT),
        name="gla",
    )(qg, kg, vg, gg, alr, wg, bg, gn)


SB_T = 128
SB_GW = 2 * LANES
SB_GROUPS = SB_HEADS * SB_DH // SB_GW
SB_TILES = 4 * SB_GROUPS


def _sb_kernel(q_ref, k_ref, v_ref, tu_ref, gn_ref, o_ref,
               acc_ref, car_ref, kb_ref, vb_ref, zl_ref, l_ref, cs_ref):
    i = pl.program_id(1)
    T = SB_T
    lane = lax.broadcasted_iota(jnp.int32, (1, SB_GW), 1)
    is_h0 = (lane % LANES) < SB_DH
    qg = []
    for g in range(SB_GROUPS):
        q = q_ref[:, g * SB_GW:(g + 1) * SB_GW]
        zero = jnp.zeros_like(q)
        qg.append(jnp.concatenate([jnp.where(is_h0, q, zero), jnp.where(is_h0, zero, q)], axis=0))
    row = lax.broadcasted_iota(jnp.int32, (T, T), 0)
    col = lax.broadcasted_iota(jnp.int32, (T, T), 1)
    acc_ref[...] = jnp.zeros_like(acc_ref)
    car_ref[...] = jnp.zeros_like(car_ref)
    kb_ref[...] = jnp.zeros_like(kb_ref)
    vb_ref[...] = jnp.zeros_like(vb_ref)

    def cond(carry):
        j, alive = carry
        return jnp.logical_and(j >= 0, alive > SB_DEAD)

    def body(carry):
        j, _ = carry
        ks = pl.multiple_of(j * T, T)
        valid = (col + (j - i) * T) < row
        for g in range(SB_GROUPS):
            for p in range(2):
                src = slice(g * SB_GW + p * LANES, g * SB_GW + (p + 1) * LANES)
                kb_ref[g, p * T:(p + 1) * T, p * LANES:(p + 1) * LANES] = k_ref[pl.ds(ks, T), src]
                vb_ref[g, p * T:(p + 1) * T, p * LANES:(p + 1) * LANES] = v_ref[pl.ds(ks, T), src]
            z = _dot_nt(qg[g], kb_ref[g])
            for h in range(2):
                for p in range(2):
                    n = 4 * g + 2 * h + p
                    zq = z[h * T:(h + 1) * T, p * T:(p + 1) * T]
                    lnb = jnp.where(valid, -(jnp.maximum(zq, 0.0) + jnp.log(1.0 + jnp.exp(-jnp.abs(zq)))), 0.0)
                    hi, lo = _split_bf16(lnb)
                    l_ref[n * T:(n + 1) * T, :T] = hi
                    l_ref[n * T:(n + 1) * T, T:] = lo
                    zl_ref[n] = zq + lnb
        cs_ref[...] = _dot(l_ref[...], tu_ref[...])
        top = None
        for g in range(SB_GROUPS):
            w_rows = []
            for h in range(2):
                w_cols = []
                for p in range(2):
                    n = 4 * g + 2 * h + p
                    car = car_ref[n]
                    w = jnp.where(valid, jnp.exp(zl_ref[n] + cs_ref[n * T:(n + 1) * T, :T] + car), 0.0)
                    w_cols.append(w.astype(BF16))
                    car = car + cs_ref[n * T:(n + 1) * T, T:]
                    car_ref[n] = car
                    top = car if top is None else jnp.maximum(top, car)
                w_rows.append(jnp.concatenate(w_cols, axis=1))
            acc_ref[g] += _dot(jnp.concatenate(w_rows, axis=0), vb_ref[g])
        return j - 1, jnp.max(top)

    lax.while_loop(cond, body, (i, jnp.float32(0.0)))

    head_of_lane = lane // SB_DH
    for g in range(SB_GROUPS):
        a = acc_ref[g]
        o = jnp.where(is_h0, a[:T], a[T:])
        sq = o * o
        ms = jnp.zeros_like(o)
        for hd in range(SB_GW // SB_DH):
            mine = head_of_lane == hd
            ms = jnp.where(mine, jnp.sum(jnp.where(mine, sq, 0.0), axis=-1, keepdims=True), ms)
        cols = slice(g * SB_GW, (g + 1) * SB_GW)
        o_ref[:, cols] = (o * lax.rsqrt(ms * (1.0 / SB_DH) + EPS) * gn_ref[:, cols]).astype(o_ref.dtype)


def _sb(qs, ks, vs, tu, gn, B, S):
    n_q = S // SB_T
    width = SB_HEADS * SB_DH
    return pl.pallas_call(
        _sb_kernel,
        grid=(B, n_q),
        in_specs=[pl.BlockSpec((SB_T, width), lambda b, i: (b * n_q + i, 0)),
                  pl.BlockSpec((S, width), lambda b, i: (b, 0)),
                  pl.BlockSpec((S, width), lambda b, i: (b, 0)),
                  pl.BlockSpec((2 * SB_T, 2 * SB_T), lambda b, i: (0, 0)),
                  pl.BlockSpec((1, width), lambda b, i: (0, 0))],
        out_specs=pl.BlockSpec((SB_T, width), lambda b, i: (b * n_q + i, 0)),
        out_shape=jax.ShapeDtypeStruct((B * S, width), BF16),
        scratch_shapes=[pltpu.VMEM((SB_GROUPS, 2 * SB_T, SB_GW), F32),
                        pltpu.VMEM((SB_TILES, SB_T, SB_T), F32),
                        pltpu.VMEM((SB_GROUPS, 2 * SB_T, SB_GW), BF16),
                        pltpu.VMEM((SB_GROUPS, 2 * SB_T, SB_GW), BF16),
                        pltpu.VMEM((SB_TILES, SB_T, SB_T), F32),
                        pltpu.VMEM((SB_TILES * SB_T, 2 * SB_T), BF16),
                        pltpu.VMEM((SB_TILES * SB_T, 2 * SB_T), F32)],
        compiler_params=pltpu.CompilerParams(dimension_semantics=("arbitrary", "arbitrary"),
                                             vmem_limit_bytes=VMEM_LIMIT),
        name="sb",
    )(qs, ks, vs, tu, gn)


def _layer_norm(y, g, b):
    mu = jnp.mean(y, axis=-1, keepdims=True)
    d = y - mu
    var = jnp.mean(d * d, axis=-1, keepdims=True)
    return d * lax.rsqrt(var + EPS) * g + b


def _outproj_kernel(og_ref, os_ref, x_ref, wo_ref, g_ref, b_ref, x1_ref, x1b_ref):
    half = og_ref.shape[1]
    mix = _dot(og_ref[...], wo_ref[:half, :]) + _dot(os_ref[...], wo_ref[half:, :])
    x1 = _layer_norm(ALPHA * x_ref[...] + mix, g_ref[...], b_ref[...])
    x1_ref[...] = x1
    x1b_ref[...] = x1.astype(BF16)


def _outproj(ogla, osb, x2d, wo, g, b, tm):
    T, D = x2d.shape
    half = ogla.shape[1]
    return pl.pallas_call(
        _outproj_kernel,
        grid=(T // tm,),
        in_specs=[pl.BlockSpec((tm, half), lambda i: (i, 0)),
                  pl.BlockSpec((tm, half), lambda i: (i, 0)),
                  pl.BlockSpec((tm, D), lambda i: (i, 0)),
                  pl.BlockSpec((2 * half, D), lambda i: (0, 0)),
                  pl.BlockSpec((1, D), lambda i: (0, 0)),
                  pl.BlockSpec((1, D), lambda i: (0, 0))],
        out_specs=[pl.BlockSpec((tm, D), lambda i: (i, 0)),
                   pl.BlockSpec((tm, D), lambda i: (i, 0))],
        out_shape=[jax.ShapeDtypeStruct((T, D), F32), jax.ShapeDtypeStruct((T, D), BF16)],
        compiler_params=pltpu.CompilerParams(dimension_semantics=("arbitrary",),
                                             vmem_limit_bytes=VMEM_LIMIT),
        name="outproj",
    )(ogla, osb, x2d, wo, g, b)


SUBLANES = 8


def _batcher_network(n):
    pairs, p = [], 1
    while p < n:
        k = p
        while k >= 1:
            for j in range(k % p, n - k, 2 * k):
                for i in range(min(k, n - j - k)):
                    if (i + j) // (2 * p) == (i + j + k) // (2 * p):
                        pairs.append((i + j, i + j + k))
            k //= 2
        p *= 2
    return pairs


def _merge_top(stacks, n):
    stacks = [list(st) for st in stacks]
    tops = []
    for r in range(n):
        head = stacks[0][0]
        for st in stacks[1:]:
            head = jnp.maximum(head, st[0])
        m = jnp.max(head, axis=0, keepdims=True)
        tops.append(m)
        need = n - r - 1
        for st in stacks:
            if need == 0:
                break
            hit = st[0] == m
            depth = min(need, len(st))
            for d in range(depth):
                below = st[d + 1] if d + 1 < len(st) else NEG_INF
                st[d] = jnp.where(hit, below, st[d])
            del st[depth:]
    return tops


def _top_rows(s, n):
    slabs = [s[r:r + SUBLANES] for r in range(0, s.shape[0], SUBLANES)]
    for i, j in _batcher_network(len(slabs)):
        slabs[i], slabs[j] = jnp.maximum(slabs[i], slabs[j]), jnp.minimum(slabs[i], slabs[j])
    return _merge_top([slabs], n)


def _peer_sel_kernel(x_ref, wq_ref, sk_ref, s1_ref, e1_ref, th_ref, c_ref, q_scr):
    tt = x_ref.shape[0]
    qry = _dot(x_ref[...], wq_ref[...]).astype(BF16)
    for hp in range(2 * PEER_HEADS):
        q_scr[hp] = qry[:, hp * PEER_HALF:(hp + 1) * PEER_HALF]
    sub = lax.broadcasted_iota(jnp.int32, (SUBLANES, tt), 0)

    def head(h, _):
        s0 = _dot_nt(sk_ref[2 * h], q_scr[2 * h])
        s1 = _dot_nt(sk_ref[2 * h + 1], q_scr[2 * h + 1])
        n = PEER_TOPK + 1
        a = _top_rows(s0, n)
        b = _top_rows(s1, n)
        a_lo = jnp.concatenate(a[:SUBLANES], axis=0)
        a_hi = jnp.concatenate(a[SUBLANES:2 * SUBLANES], axis=0)
        xs = [jnp.where(sub < n // l, a_lo + b[l - 1], NEG_INF) for l in range(1, n + 1)]
        zs = [a_hi + b[0], jnp.where(sub < 1, a[n - 1] + b[0], NEG_INF)]
        best = _merge_top([xs, zs], n)
        z = jnp.zeros_like(best[0])
        for r in range(PEER_TOPK):
            z = z + jnp.exp(best[r] - best[0])
        tau = 0.5 * (best[PEER_TOPK - 1] + best[PEER_TOPK])
        s1_ref[0, h] = s1
        e1_ref[0, h] = jnp.exp(s1 - b[0])
        th_ref[0, h] = tau - s0
        c_ref[0, h] = jnp.exp(s0 - a[0]) / z
        return 0

    lax.fori_loop(0, PEER_HEADS, head, 0)


def _peer_sel(x1b, wq, sk, tt):
    T, D = x1b.shape
    qd = wq.shape[1]
    shp = jax.ShapeDtypeStruct((T // tt, PEER_HEADS, PEER_NKEYS, tt), F32)
    ospec = pl.BlockSpec((1, PEER_HEADS, PEER_NKEYS, tt), lambda i: (i, 0, 0, 0))
    return pl.pallas_call(
        _peer_sel_kernel,
        grid=(T // tt,),
        in_specs=[pl.BlockSpec((tt, D), lambda i: (i, 0)),
                  pl.BlockSpec((D, qd), lambda i: (0, 0)),
                  pl.BlockSpec((2 * PEER_HEADS, PEER_NKEYS, PEER_HALF), lambda i: (0, 0, 0))],
        out_specs=[ospec, ospec, ospec, ospec],
        out_shape=[shp, shp, shp, shp],
        scratch_shapes=[pltpu.VMEM((2 * PEER_HEADS, tt, PEER_HALF), BF16)],
        compiler_params=pltpu.CompilerParams(dimension_semantics=("arbitrary",),
                                             vmem_limit_bytes=VMEM_LIMIT),
        name="peer_sel",
    )(x1b, wq, sk)


PEER_ET = 1024
PEER_EQ = 256
PEER_LC = 256


def _peer_ffn_kernel(xt_ref, x1_ref, u_ref, vt_ref, s1_ref, e1_ref, th_ref, c_ref, g_ref, b_ref,
                     o_ref, acc_ref, h_ref, p_ref):
    e = pl.program_id(1)
    n_lc = xt_ref.shape[0]
    n_q = PEER_ET // PEER_EQ
    n_k = PEER_EQ // PEER_NKEYS
    assert n_lc == 2

    @pl.when(e == 0)
    def _():
        acc_ref[...] = jnp.zeros_like(acc_ref)

    def scores(q, lc):
        r0 = q * PEER_EQ if isinstance(q, int) else pl.multiple_of(q * PEER_EQ, PEER_EQ)
        h_ref[lc] = _dot(u_ref[pl.ds(r0, PEER_EQ), :], xt_ref[lc])

    def gated_act(q, lc):
        for k in range(n_k):
            rows = slice(k * PEER_NKEYS, (k + 1) * PEER_NKEYS)
            gate = jnp.zeros((PEER_NKEYS, PEER_LC), F32)
            for h in range(PEER_HEADS):
                th = th_ref[lc, h, pl.ds(q * n_k + k, 1), :]
                cc = c_ref[lc, h, pl.ds(q * n_k + k, 1), :]
                gate = gate + jnp.where(s1_ref[lc, h] >= th, e1_ref[lc, h] * cc, 0.0)
            hh = h_ref[lc, rows, :]
            act = 0.5 * hh * (1.0 + lax.erf(hh * (2.0 ** -0.5)))
            p_ref[lc, rows, :] = (gate * act).astype(BF16)

    def accumulate(q, lc):
        acc_ref[lc] += _dot(vt_ref[q], p_ref[lc])

    scores(0, 0)
    scores(0, 1)
    gated_act(0, 0)

    def step(q, _):
        scores(q + 1, 0)
        gated_act(q, 1)
        accumulate(q, 0)
        scores(q + 1, 1)
        gated_act(q + 1, 0)
        accumulate(q, 1)
        return 0

    lax.fori_loop(0, n_q - 1, step, 0)
    gated_act(n_q - 1, 1)
    accumulate(n_q - 1, 0)
    accumulate(n_q - 1, 1)

    @pl.when(e == pl.num_programs(1) - 1)
    def _():
        for lc in range(n_lc):
            rows = slice(lc * PEER_LC, (lc + 1) * PEER_LC)
            y = ALPHA * x1_ref[rows, :] + acc_ref[lc].T
            o_ref[rows, :] = _layer_norm(y, g_ref[...], b_ref[...])


def _peer_ffn(xt, x1, u, vt, s1, e1, th, cc, g, b, tt):
    T, D = x1.shape
    E = u.shape[0]
    n_lc = tt // PEER_LC
    n_q = PEER_ET // PEER_EQ
    tok = pl.BlockSpec((n_lc, PEER_HEADS, PEER_NKEYS, PEER_LC), lambda t, e: (t, 0, 0, 0))
    sel = pl.BlockSpec((n_lc, PEER_HEADS, PEER_ET // PEER_NKEYS, PEER_LC), lambda t, e: (t, 0, e, 0))
    return pl.pallas_call(
        _peer_ffn_kernel,
        grid=(T // tt, E // PEER_ET),
        in_specs=[pl.BlockSpec((n_lc, D, PEER_LC), lambda t, e: (t, 0, 0)),
                  pl.BlockSpec((tt, D), lambda t, e: (t, 0)),
                  pl.BlockSpec((PEER_ET, D), lambda t, e: (e, 0)),
                  pl.BlockSpec((n_q, D, PEER_EQ), lambda t, e: (e, 0, 0)),
                  tok, tok, sel, sel,
                  pl.BlockSpec((1, D), lambda t, e: (0, 0)),
                  pl.BlockSpec((1, D), lambda t, e: (0, 0))],
        out_specs=pl.BlockSpec((tt, D), lambda t, e: (t, 0)),
        out_shape=jax.ShapeDtypeStruct((T, D), F32),
        scratch_shapes=[pltpu.VMEM((n_lc, D, PEER_LC), F32),
                        pltpu.VMEM((n_lc, PEER_EQ, PEER_LC), F32),
                        pltpu.VMEM((n_lc, PEER_EQ, PEER_LC), BF16)],
        compiler_params=pltpu.CompilerParams(dimension_semantics=("arbitrary", "arbitrary"),
                                             vmem_limit_bytes=VMEM_LIMIT),
        name="peer_ffn",
    )(xt, x1, u, vt, s1, e1, th, cc, g, b)


def _layer(x, w_in, w_gla_gate, b_gla_gate, gla_norm_g, sb_norm_g, w_out, ln1_g, ln1_b,
           peer_w_query, peer_sub_keys, peer_u, peer_v, ln2_g, ln2_b):
    B, S, D = x.shape
    T = B * S
    x2d = x.reshape(T, D)

    kw, gw, sw = GLA_HEADS * GLA_DK, GLA_HEADS * GLA_DV, SB_HEADS * SB_DH
    pts = [0, kw, 2 * kw, 2 * kw + gw, 2 * kw + 2 * gw, 2 * kw + 2 * gw + GLA_RANK]
    pts += [pts[-1] + sw, pts[-1] + 2 * sw, pts[-1] + 3 * sw]
    cols = [w_in[:, pts[n]:pts[n + 1]] for n in range(8)]
    cols[4] = jnp.pad(cols[4], ((0, 0), (0, LANES - GLA_RANK)))
    w_cat = jnp.concatenate(cols, axis=1).astype(BF16)
    wg = jnp.pad(w_gla_gate, ((0, LANES - GLA_RANK), (0, 0))).astype(BF16)

    qg, kg, vg, gg, alr, qs, ks, vs = _inproj(x2d, w_cat, min(512, T))

    ogla = _gla(qg, kg, vg, gg, alr, wg, b_gla_gate.reshape(1, -1), gla_norm_g.reshape(1, -1),
                B, S, min(512, S))

    r = lax.broadcasted_iota(jnp.int32, (2 * SB_T, 2 * SB_T), 0) % SB_T
    c = lax.broadcasted_iota(jnp.int32, (2 * SB_T, 2 * SB_T), 1)
    tu = jnp.where((c >= SB_T) | (r > c), 1.0, 0.0).astype(BF16)
    osb = _sb(qs, ks, vs, tu, sb_norm_g.reshape(1, -1), B, S)

    x1, x1b = _outproj(ogla, osb, x2d, w_out.astype(BF16), ln1_g.reshape(1, D), ln1_b.reshape(1, D),
                       min(512, T))

    sk = peer_sub_keys.reshape(2 * PEER_HEADS, PEER_NKEYS, PEER_HALF).astype(BF16)
    s1, e1, th, cc = _peer_sel(x1b, peer_w_query.astype(BF16), sk, PEER_LC)

    xt = x1b.reshape(-1, PEER_LC, D).transpose(0, 2, 1)
    vt = peer_v.reshape(-1, PEER_EQ, D).transpose(0, 2, 1).astype(BF16)
    out = _peer_ffn(xt, x1, peer_u.astype(BF16), vt, s1, e1, th, cc,
                    ln2_g.reshape(1, D), ln2_b.reshape(1, D), 2 * PEER_LC)
    return out.reshape(B, S, D)


def kernel(x, w_in, w_gla_gate, b_gla_gate, gla_norm_g, sb_norm_g, w_out, ln1_g, ln1_b,
           peer_w_query, peer_sub_keys, peer_u, peer_v, ln2_g, ln2_b):
    for l in range(DEPTH):
        x = _layer(x, w_in[l], w_gla_gate[l], b_gla_gate[l], gla_norm_g[l], sb_norm_g[l], w_out[l],
                   ln1_g[l], ln1_b[l], peer_w_query[l], peer_sub_keys[l], peer_u[l], peer_v[l],
                   ln2_g[l], ln2_b[l])
    return x
```

```python
import functools

import jax
import jax.numpy as jnp
from jax import lax
from jax.experimental import pallas as pl
from jax.experimental.pallas import tpu as pltpu

F32 = jnp.float32
BF16 = jnp.bfloat16

LANES = 128
GLA_HEADS = 4
GLA_DK = 64
GLA_DV = 128
GLA_RANK = 16
GLA_GATE_TEMP = 16.0
GLA_CHUNK = 64
SB_HEADS = 8
SB_DH = 64
PEER_HEADS = 8
PEER_NKEYS = 128
PEER_HALF = 128
PEER_TOPK = 16
PEER_QONE = 16384.0
DEPTH = 1
ALPHA = (2.0 * DEPTH) ** 0.25
EPS = 1e-5
NEG_INF = float("-inf")
SB_DEAD = -104.0

VMEM_LIMIT = 56 * 1024 * 1024


def _dot(a, b):
    return jnp.dot(a, b, preferred_element_type=F32)


def _dot_nt(a, b):
    return lax.dot_general(a, b, (((1,), (1,)), ((), ())), preferred_element_type=F32)


def _dot_tn(a, b):
    return lax.dot_general(a, b, (((0,), (0,)), ((), ())), preferred_element_type=F32)


def _split_bf16(v):
    hi = v.astype(BF16)
    lo = (v - hi.astype(F32)).astype(BF16)
    return hi, lo


def _log_sigmoid(z):
    return jnp.minimum(z, 0.0) - jnp.log(1.0 + jnp.exp(-jnp.abs(z)))


IN_SPLITS = (("qg", 256, F32), ("kg", 256, F32), ("vg", 512, BF16), ("gg", 512, F32),
             ("alr", LANES, F32), ("qs", 512, BF16), ("ks", 512, BF16), ("vs", 512, BF16))


def _inproj_kernel(x_ref, w_ref, *out_refs):
    xb = x_ref[...].astype(BF16)
    off = 0
    for (name, width, _), o_ref in zip(IN_SPLITS, out_refs):
        r = _dot(xb, w_ref[:, off:off + width])
        if name in ("qg", "qs"):
            r = r * (GLA_DK ** -0.5)
        o_ref[...] = r.astype(o_ref.dtype)
        off += width


def _inproj(x2d, w_cat, tm):
    T, D = x2d.shape
    wtot = w_cat.shape[1]
    return pl.pallas_call(
        _inproj_kernel,
        grid=(T // tm,),
        in_specs=[pl.BlockSpec((tm, D), lambda i: (i, 0)),
                  pl.BlockSpec((D, wtot), lambda i: (0, 0))],
        out_specs=[pl.BlockSpec((tm, w), lambda i: (i, 0)) for _, w, _ in IN_SPLITS],
        out_shape=[jax.ShapeDtypeStruct((T, w), dt) for _, w, dt in IN_SPLITS],
        compiler_params=pltpu.CompilerParams(dimension_semantics=("arbitrary",),
                                             vmem_limit_bytes=VMEM_LIMIT),
        name="inproj",
    )(x2d, w_cat)


def _gla_kernel(q_ref, k_ref, v_ref, g_ref, a_ref, wg_ref, bg_ref, gn_ref, o_ref, st_ref, *, n_chunks):
    @pl.when(pl.program_id(2) == 0)
    def _():
        st_ref[...] = jnp.zeros_like(st_ref)

    C = GLA_CHUNK
    lane = lax.broadcasted_iota(jnp.int32, (1, 2 * GLA_DK), 1)
    head_mask = [lane < GLA_DK, lane >= GLA_DK]
    r_i = lax.broadcasted_iota(jnp.int32, (C, C), 0)
    c_i = lax.broadcasted_iota(jnp.int32, (C, C), 1)
    causal = c_i <= r_i
    tri = jnp.where(causal, 1.0, 0.0).astype(BF16)

    la_all = _log_sigmoid(_dot(a_ref[...].astype(BF16), wg_ref[...]) + bg_ref[...]) * (1.0 / GLA_GATE_TEMP)

    for c in range(n_chunks):
        rows = slice(c * C, (c + 1) * C)
        la = la_all[rows]
        hi, lo = _split_bf16(la)
        b = _dot(tri, hi) + _dot(tri, lo)
        b_last = b[C - 1:C, :]
        q = q_ref[rows, :]
        k = k_ref[rows, :]
        q_in = q * jnp.exp(b)
        k_in = (k * jnp.exp(-b)).astype(BF16)
        k_st = (k * jnp.exp(b_last - b)).astype(BF16)
        dec = jnp.exp(b_last)
        v = v_ref[rows, :]
        st = st_ref[...]
        st_b = st.astype(BF16)
        outs = []
        for h in range(2):
            qm = jnp.where(head_mask[h], q_in, 0.0).astype(BF16)
            att = jnp.where(causal, _dot_nt(qm, k_in), 0.0)
            vh = v[:, h * GLA_DV:(h + 1) * GLA_DV]
            o_h = _dot(att.astype(BF16), vh) + _dot_nt(qm, st_b[h * GLA_DV:(h + 1) * GLA_DV, :])
            ms = jnp.mean(o_h * o_h, axis=-1, keepdims=True)
            outs.append(o_h * lax.rsqrt(ms + EPS))
        st_ref[...] = st * dec + _dot_tn(v, k_st)
        o = jnp.concatenate(outs, axis=1) * gn_ref[...]
        g = g_ref[rows, :]
        o_ref[rows, :] = (o * (g * jax.nn.sigmoid(g))).astype(o_ref.dtype)


def _gla(qg, kg, vg, gg, alr, wg, bg, gn, B, S, ls):
    n_s = S // ls
    row = lambda b, hp, s: (b * n_s + s, hp)
    row0 = lambda b, hp, s: (b * n_s + s, 0)
    par = lambda b, hp, s: (0, hp)
    return pl.pallas_call(
        functools.partial(_gla_kernel, n_chunks=ls // GLA_CHUNK),
        grid=(B, GLA_HEADS // 2, n_s),
        in_specs=[pl.BlockSpec((ls, 2 * GLA_DK), row),
                  pl.BlockSpec((ls, 2 * GLA_DK), row),
                  pl.BlockSpec((ls, 2 * GLA_DV), row),
                  pl.BlockSpec((ls, 2 * GLA_DV), row),
                  pl.BlockSpec((ls, LANES), row0),
                  pl.BlockSpec((LANES, 2 * GLA_DK), par),
                  pl.BlockSpec((1, 2 * GLA_DK), par),
                  pl.BlockSpec((1, 2 * GLA_DV), par)],
        out_specs=pl.BlockSpec((ls, 2 * GLA_DV), row),
        out_shape=jax.ShapeDtypeStruct((B * S, GLA_HEADS * GLA_DV), BF16),
        scratch_shapes=[pltpu.VMEM((2 * GLA_DV, 2 * GLA_DK), F32)],
        compiler_params=pltpu.CompilerParams(dimension_semantics=("arbitrary", "arbitrary", "arbitrary"),
                                             vmem_limit_bytes=VMEM_LIMIT),
        name="gla",
    )(qg, kg, vg, gg, alr, wg, bg, gn)


SB_T = 128
SB_GW = 2 * LANES
SB_GROUPS = SB_HEADS * SB_DH // SB_GW
SB_TILES = 4 * SB_GROUPS


def _sb_kernel(q_ref, k_ref, v_ref, tu_ref, gn_ref, o_ref,
               acc_ref, car_ref, kb_ref, vb_ref, zl_ref, l_ref, cs_ref):
    i = pl.program_id(1)
    T = SB_T
    lane = lax.broadcasted_iota(jnp.int32, (1, SB_GW), 1)
    is_h0 = (lane % LANES) < SB_DH
    qg = []
    for g in range(SB_GROUPS):
        q = q_ref[:, g * SB_GW:(g + 1) * SB_GW]
        zero = jnp.zeros_like(q)
        qg.append(jnp.concatenate([jnp.where(is_h0, q, zero), jnp.where(is_h0, zero, q)], axis=0))
    row = lax.broadcasted_iota(jnp.int32, (T, T), 0)
    col = lax.broadcasted_iota(jnp.int32, (T, T), 1)
    acc_ref[...] = jnp.zeros_like(acc_ref)
    car_ref[...] = jnp.zeros_like(car_ref)
    kb_ref[...] = jnp.zeros_like(kb_ref)
    vb_ref[...] = jnp.zeros_like(vb_ref)

    def cond(carry):
        j, alive = carry
        return jnp.logical_and(j >= 0, alive > SB_DEAD)

    def body(carry):
        j, _ = carry
        ks = pl.multiple_of(j * T, T)
        valid = (col + (j - i) * T) < row
        for g in range(SB_GROUPS):
            for p in range(2):
                src = slice(g * SB_GW + p * LANES, g * SB_GW + (p + 1) * LANES)
                kb_ref[g, p * T:(p + 1) * T, p * LANES:(p + 1) * LANES] = k_ref[pl.ds(ks, T), src]
                vb_ref[g, p * T:(p + 1) * T, p * LANES:(p + 1) * LANES] = v_ref[pl.ds(ks, T), src]
            z = _dot_nt(qg[g], kb_ref[g])
            for h in range(2):
                for p in range(2):
                    n = 4 * g + 2 * h + p
                    zq = z[h * T:(h + 1) * T, p * T:(p + 1) * T]
                    lnb = jnp.where(valid, -(jnp.maximum(zq, 0.0) + jnp.log(1.0 + jnp.exp(-jnp.abs(zq)))), 0.0)
                    hi, lo = _split_bf16(lnb)
                    l_ref[n * T:(n + 1) * T, :T] = hi
                    l_ref[n * T:(n + 1) * T, T:] = lo
                    zl_ref[n] = zq + lnb
        cs_ref[...] = _dot(l_ref[...], tu_ref[...])
        top = None
        for g in range(SB_GROUPS):
            w_rows = []
            for h in range(2):
                w_cols = []
                for p in range(2):
                    n = 4 * g + 2 * h + p
                    car = car_ref[n]
                    w = jnp.where(valid, jnp.exp(zl_ref[n] + cs_ref[n * T:(n + 1) * T, :T] + car), 0.0)
                    w_cols.append(w.astype(BF16))
                    car = car + cs_ref[n * T:(n + 1) * T, T:]
                    car_ref[n] = car
                    top = car if top is None else jnp.maximum(top, car)
                w_rows.append(jnp.concatenate(w_cols, axis=1))
            acc_ref[g] += _dot(jnp.concatenate(w_rows, axis=0), vb_ref[g])
        return j - 1, jnp.max(top)

    lax.while_loop(cond, body, (i, jnp.float32(0.0)))

    head_of_lane = lane // SB_DH
    for g in range(SB_GROUPS):
        a = acc_ref[g]
        o = jnp.where(is_h0, a[:T], a[T:])
        sq = o * o
        ms = jnp.zeros_like(o)
        for hd in range(SB_GW // SB_DH):
            mine = head_of_lane == hd
            ms = jnp.where(mine, jnp.sum(jnp.where(mine, sq, 0.0), axis=-1, keepdims=True), ms)
        cols = slice(g * SB_GW, (g + 1) * SB_GW)
        o_ref[:, cols] = (o * lax.rsqrt(ms * (1.0 / SB_DH) + EPS) * gn_ref[:, cols]).astype(o_ref.dtype)


def _sb(qs, ks, vs, tu, gn, B, S):
    n_q = S // SB_T
    width = SB_HEADS * SB_DH
    return pl.pallas_call(
        _sb_kernel,
        grid=(B, n_q),
        in_specs=[pl.BlockSpec((SB_T, width), lambda b, i: (b * n_q + i, 0)),
                  pl.BlockSpec((S, width), lambda b, i: (b, 0)),
                  pl.BlockSpec((S, width), lambda b, i: (b, 0)),
                  pl.BlockSpec((2 * SB_T, 2 * SB_T), lambda b, i: (0, 0)),
                  pl.BlockSpec((1, width), lambda b, i: (0, 0))],
        out_specs=pl.BlockSpec((SB_T, width), lambda b, i: (b * n_q + i, 0)),
        out_shape=jax.ShapeDtypeStruct((B * S, width), BF16),
        scratch_shapes=[pltpu.VMEM((SB_GROUPS, 2 * SB_T, SB_GW), F32),
                        pltpu.VMEM((SB_TILES, SB_T, SB_T), F32),
                        pltpu.VMEM((SB_GROUPS, 2 * SB_T, SB_GW), BF16),
                        pltpu.VMEM((SB_GROUPS, 2 * SB_T, SB_GW), BF16),
                        pltpu.VMEM((SB_TILES, SB_T, SB_T), F32),
                        pltpu.VMEM((SB_TILES * SB_T, 2 * SB_T), BF16),
                        pltpu.VMEM((SB_TILES * SB_T, 2 * SB_T), F32)],
        compiler_params=pltpu.CompilerParams(dimension_semantics=("arbitrary", "arbitrary"),
                                             vmem_limit_bytes=VMEM_LIMIT),
        name="sb",
    )(qs, ks, vs, tu, gn)


def _layer_norm(y, g, b):
    mu = jnp.mean(y, axis=-1, keepdims=True)
    d = y - mu
    var = jnp.mean(d * d, axis=-1, keepdims=True)
    return d * lax.rsqrt(var + EPS) * g + b


def _outproj_kernel(og_ref, os_ref, x_ref, wo_ref, g_ref, b_ref, x1_ref, x1b_ref):
    half = og_ref.shape[1]
    mix = _dot(og_ref[...], wo_ref[:half, :]) + _dot(os_ref[...], wo_ref[half:, :])
    x1 = _layer_norm(ALPHA * x_ref[...] + mix, g_ref[...], b_ref[...])
    x1_ref[...] = x1
    x1b_ref[...] = x1.astype(BF16)


def _outproj(ogla, osb, x2d, wo, g, b, tm):
    T, D = x2d.shape
    half = ogla.shape[1]
    return pl.pallas_call(
        _outproj_kernel,
        grid=(T // tm,),
        in_specs=[pl.BlockSpec((tm, half), lambda i: (i, 0)),
                  pl.BlockSpec((tm, half), lambda i: (i, 0)),
                  pl.BlockSpec((tm, D), lambda i: (i, 0)),
                  pl.BlockSpec((2 * half, D), lambda i: (0, 0)),
                  pl.BlockSpec((1, D), lambda i: (0, 0)),
                  pl.BlockSpec((1, D), lambda i: (0, 0))],
        out_specs=[pl.BlockSpec((tm, D), lambda i: (i, 0)),
                   pl.BlockSpec((tm, D), lambda i: (i, 0))],
        out_shape=[jax.ShapeDtypeStruct((T, D), F32), jax.ShapeDtypeStruct((T, D), BF16)],
        compiler_params=pltpu.CompilerParams(dimension_semantics=("arbitrary",),
                                             vmem_limit_bytes=VMEM_LIMIT),
        name="outproj",
    )(ogla, osb, x2d, wo, g, b)


SUBLANES = 8


def _batcher_network(n):
    pairs, p = [], 1
    while p < n:
        k = p
        while k >= 1:
            for j in range(k % p, n - k, 2 * k):
                for i in range(min(k, n - j - k)):
                    if (i + j) // (2 * p) == (i + j + k) // (2 * p):
                        pairs.append((i + j, i + j + k))
            k //= 2
        p *= 2
    return pairs


def _merge_top(stacks, n):
    stacks = [list(st) for st in stacks]
    tops = []
    for r in range(n):
        head = stacks[0][0]
        for st in stacks[1:]:
            head = jnp.maximum(head, st[0])
        m = jnp.max(head, axis=0, keepdims=True)
        tops.append(m)
        need = n - r - 1
        for st in stacks:
            if need == 0:
                break
            hit = st[0] == m
            depth = min(need, len(st))
            for d in range(depth):
                below = st[d + 1] if d + 1 < len(st) else NEG_INF
                st[d] = jnp.where(hit, below, st[d])
            del st[depth:]
    return tops


def _top_rows(s, n):
    slabs = [s[r:r + SUBLANES] for r in range(0, s.shape[0], SUBLANES)]
    for i, j in _batcher_network(len(slabs)):
        slabs[i], slabs[j] = jnp.maximum(slabs[i], slabs[j]), jnp.minimum(slabs[i], slabs[j])
    return _merge_top([slabs], n)


def _peer_sel_kernel(x_ref, wq_ref, sk_ref, s1_ref, e1_ref, th_ref, c_ref, q_scr):
    tt = x_ref.shape[0]
    qry = _dot(x_ref[...], wq_ref[...]).astype(BF16)
    for hp in range(2 * PEER_HEADS):
        q_scr[hp] = qry[:, hp * PEER_HALF:(hp + 1) * PEER_HALF]
    sub = lax.broadcasted_iota(jnp.int32, (SUBLANES, tt), 0)

    def head(h, _):
        s0 = _dot_nt(sk_ref[2 * h], q_scr[2 * h])
        s1 = _dot_nt(sk_ref[2 * h + 1], q_scr[2 * h + 1])
        n = PEER_TOPK + 1
        a = _top_rows(s0, n)
        b = _top_rows(s1, n)
        a_lo = jnp.concatenate(a[:SUBLANES], axis=0)
        a_hi = jnp.concatenate(a[SUBLANES:2 * SUBLANES], axis=0)
        xs = [jnp.where(sub < n // l, a_lo + b[l - 1], NEG_INF) for l in range(1, n + 1)]
        zs = [a_hi + b[0], jnp.where(sub < 1, a[n - 1] + b[0], NEG_INF)]
        best = _merge_top([xs, zs], n)
        z = jnp.zeros_like(best[0])
        for r in range(PEER_TOPK):
            z = z + jnp.exp(best[r] - best[0])
        tau = 0.5 * (best[PEER_TOPK - 1] + best[PEER_TOPK])
        d1 = s1 - b[0]
        d0 = s0 - a[0]
        tau_rel = tau - best[0]
        scale = PEER_QONE / jnp.maximum(-tau_rel, 1e-30)
        s1_ref[0, h] = jnp.clip(d1 * scale, -2.0 * PEER_QONE, 0.0).astype(jnp.int32).astype(jnp.int16)
        th_ref[0, h] = jnp.clip((tau_rel - d0) * scale, -2.0 * PEER_QONE, 2.0 * PEER_QONE - 1.0).astype(jnp.int32)
        e1_ref[0, h] = jnp.exp(d1).astype(BF16)
        c_ref[0, h] = (0.5 * jnp.exp(d0)) / z
        return 0

    lax.fori_loop(0, PEER_HEADS, head, 0)


def _peer_sel(x1b, wq, sk, tt):
    T, D = x1b.shape
    qd = wq.shape[1]
    shp = [jax.ShapeDtypeStruct((T // tt, PEER_HEADS, PEER_NKEYS, tt), dt)
           for dt in (jnp.int16, BF16, jnp.int32, F32)]
    ospec = pl.BlockSpec((1, PEER_HEADS, PEER_NKEYS, tt), lambda i: (i, 0, 0, 0))
    return pl.pallas_call(
        _peer_sel_kernel,
        grid=(T // tt,),
        in_specs=[pl.BlockSpec((tt, D), lambda i: (i, 0)),
                  pl.BlockSpec((D, qd), lambda i: (0, 0)),
                  pl.BlockSpec((2 * PEER_HEADS, PEER_NKEYS, PEER_HALF), lambda i: (0, 0, 0))],
        out_specs=[ospec, ospec, ospec, ospec],
        out_shape=shp,
        scratch_shapes=[pltpu.VMEM((2 * PEER_HEADS, tt, PEER_HALF), BF16)],
        compiler_params=pltpu.CompilerParams(dimension_semantics=("arbitrary",),
                                             vmem_limit_bytes=VMEM_LIMIT),
        name="peer_sel",
    )(x1b, wq, sk)


PEER_ET = 2048
PEER_EQ = 512
PEER_LC = 256


def _peer_ffn_kernel(xt_ref, x1_ref, u_ref, vt_ref, s1_ref, e1_ref, th_ref, c_ref, g_ref, b_ref,
                     o_ref, acc_ref, h_ref, p_ref):
    e = pl.program_id(1)
    n_lc = xt_ref.shape[0]
    n_q = PEER_ET // PEER_EQ
    n_k = PEER_EQ // PEER_NKEYS
    assert n_lc == 2

    @pl.when(e == 0)
    def _():
        acc_ref[...] = jnp.zeros_like(acc_ref)

    def scores(q, lc):
        r0 = q * PEER_EQ if isinstance(q, int) else pl.multiple_of(q * PEER_EQ, PEER_EQ)
        h_ref[lc] = _dot(u_ref[pl.ds(r0, PEER_EQ), :], xt_ref[lc])

    def gated_act(q, lc):
        for k in range(n_k):
            rows = slice(k * PEER_NKEYS, (k + 1) * PEER_NKEYS)
            gate = jnp.zeros((PEER_NKEYS, PEER_LC), BF16)
            for h in range(PEER_HEADS):
                th = th_ref[lc, h, pl.ds(q * n_k + k, 1), :].astype(jnp.int16)
                cc = c_ref[lc, h, pl.ds(q * n_k + k, 1), :].astype(BF16)
                val = e1_ref[lc, h] * cc
                gate = gate + jnp.where(s1_ref[lc, h] >= th, val, jnp.zeros_like(val))
            hh = h_ref[lc, rows, :]
            act = hh + hh * lax.erf(hh * (2.0 ** -0.5))
            p_ref[lc, rows, :] = gate * act.astype(BF16)

    def accumulate(q, lc):
        acc_ref[lc] += _dot(vt_ref[q], p_ref[lc])

    scores(0, 0)
    scores(0, 1)
    gated_act(0, 0)

    def step(q, _):
        scores(q + 1, 0)
        gated_act(q, 1)
        accumulate(q, 0)
        scores(q + 1, 1)
        gated_act(q + 1, 0)
        accumulate(q, 1)
        return 0

    lax.fori_loop(0, n_q - 1, step, 0)
    gated_act(n_q - 1, 1)
    accumulate(n_q - 1, 0)
    accumulate(n_q - 1, 1)

    @pl.when(e == pl.num_programs(1) - 1)
    def _():
        for lc in range(n_lc):
            rows = slice(lc * PEER_LC, (lc + 1) * PEER_LC)
            y = ALPHA * x1_ref[rows, :] + acc_ref[lc].T
            o_ref[rows, :] = _layer_norm(y, g_ref[...], b_ref[...])


def _peer_ffn(xt, x1, u, vt, s1, e1, th, cc, g, b, tt):
    T, D = x1.shape
    E = u.shape[0]
    n_lc = tt // PEER_LC
    n_q = PEER_ET // PEER_EQ
    tok = pl.BlockSpec((n_lc, PEER_HEADS, PEER_NKEYS, PEER_LC), lambda t, e: (t, 0, 0, 0))
    sel = pl.BlockSpec((n_lc, PEER_HEADS, PEER_ET // PEER_NKEYS, PEER_LC), lambda t, e: (t, 0, e, 0))
    return pl.pallas_call(
        _peer_ffn_kernel,
        grid=(T // tt, E // PEER_ET),
        in_specs=[pl.BlockSpec((n_lc, D, PEER_LC), lambda t, e: (t, 0, 0)),
                  pl.BlockSpec((tt, D), lambda t, e: (t, 0)),
                  pl.BlockSpec((PEER_ET, D), lambda t, e: (e, 0)),
                  pl.BlockSpec((n_q, D, PEER_EQ), lambda t, e: (e, 0, 0)),
                  tok, tok, sel, sel,
                  pl.BlockSpec((1, D), lambda t, e: (0, 0)),
                  pl.BlockSpec((1, D), lambda t, e: (0, 0))],
        out_specs=pl.BlockSpec((tt, D), lambda t, e: (t, 0)),
        out_shape=jax.ShapeDtypeStruct((T, D), F32),
        scratch_shapes=[pltpu.VMEM((n_lc, D, PEER_LC), F32),
                        pltpu.VMEM((n_lc, PEER_EQ, PEER_LC), F32),
                        pltpu.VMEM((n_lc, PEER_EQ, PEER_LC), BF16)],
        compiler_params=pltpu.CompilerParams(dimension_semantics=("arbitrary", "arbitrary"),
                                             vmem_limit_bytes=VMEM_LIMIT),
        name="peer_ffn",
    )(xt, x1, u, vt, s1, e1, th, cc, g, b)


def _layer(x, w_in, w_gla_gate, b_gla_gate, gla_norm_g, sb_norm_g, w_out, ln1_g, ln1_b,
           peer_w_query, peer_sub_keys, peer_u, peer_v, ln2_g, ln2_b):
    B, S, D = x.shape
    T = B * S
    x2d = x.reshape(T, D)

    kw, gw, sw = GLA_HEADS * GLA_DK, GLA_HEADS * GLA_DV, SB_HEADS * SB_DH
    pts = [0, kw, 2 * kw, 2 * kw + gw, 2 * kw + 2 * gw, 2 * kw + 2 * gw + GLA_RANK]
    pts += [pts[-1] + sw, pts[-1] + 2 * sw, pts[-1] + 3 * sw]
    cols = [w_in[:, pts[n]:pts[n + 1]] for n in range(8)]
    cols[4] = jnp.pad(cols[4], ((0, 0), (0, LANES - GLA_RANK)))
    w_cat = jnp.concatenate(cols, axis=1).astype(BF16)
    wg = jnp.pad(w_gla_gate, ((0, LANES - GLA_RANK), (0, 0))).astype(BF16)

    qg, kg, vg, gg, alr, qs, ks, vs = _inproj(x2d, w_cat, min(512, T))

    ogla = _gla(qg, kg, vg, gg, alr, wg, b_gla_gate.reshape(1, -1), gla_norm_g.reshape(1, -1),
                B, S, min(512, S))

    r = lax.broadcasted_iota(jnp.int32, (2 * SB_T, 2 * SB_T), 0) % SB_T
    c = lax.broadcasted_iota(jnp.int32, (2 * SB_T, 2 * SB_T), 1)
    tu = jnp.where((c >= SB_T) | (r > c), 1.0, 0.0).astype(BF16)
    osb = _sb(qs, ks, vs, tu, sb_norm_g.reshape(1, -1), B, S)

    x1, x1b = _outproj(ogla, osb, x2d, w_out.astype(BF16), ln1_g.reshape(1, D), ln1_b.reshape(1, D),
                       min(512, T))

    sk = peer_sub_keys.reshape(2 * PEER_HEADS, PEER_NKEYS, PEER_HALF).astype(BF16)
    s1, e1, th, cc = _peer_sel(x1b, peer_w_query.astype(BF16), sk, PEER_LC)

    xt = x1b.reshape(-1, PEER_LC, D).transpose(0, 2, 1)
    vt = peer_v.reshape(-1, PEER_EQ, D).transpose(0, 2, 1).astype(BF16)
    out = _peer_ffn(xt, x1, peer_u.astype(BF16), vt, s1, e1, th, cc,
                    ln2_g.reshape(1, D), ln2_b.reshape(1, D), 2 * PEER_LC)
    return out.reshape(B, S, D)


def kernel(x, w_in, w_gla_gate, b_gla_gate, gla_norm_g, sb_norm_g, w_out, ln1_g, ln1_b,
           peer_w_query, peer_sub_keys, peer_u, peer_v, ln2_g, ln2_b):
    for l in range(DEPTH):
        x = _layer(x, w_in[l], w_gla_gate[l], b_gla_gate[l], gla_norm_g[l], sb_norm_g[l], w_out[l],
                   ln1_g[l], ln1_b[l], peer_w_query[l], peer_sub_keys[l], peer_u[l], peer_v[l],
                   ln2_g[l], ln2_b[l])
    return x
```

```python
import functools

import jax
import jax.numpy as jnp
from jax import lax
from jax.experimental import pallas as pl
from jax.experimental.pallas import tpu as pltpu

F32 = jnp.float32
BF16 = jnp.bfloat16

LANES = 128
GLA_HEADS = 4
GLA_DK = 64
GLA_DV = 128
GLA_RANK = 16
GLA_GATE_TEMP = 16.0
GLA_CHUNK = 64
SB_HEADS = 8
SB_DH = 64
PEER_HEADS = 8
PEER_NKEYS = 128
PEER_HALF = 128
PEER_TOPK = 16
PEER_QONE = 16384.0
DEPTH = 1
ALPHA = (2.0 * DEPTH) ** 0.25
EPS = 1e-5
NEG_INF = float("-inf")
SB_DEAD = -104.0

VMEM_LIMIT = 56 * 1024 * 1024


def _dot(a, b):
    return jnp.dot(a, b, preferred_element_type=F32)


def _dot_nt(a, b):
    return lax.dot_general(a, b, (((1,), (1,)), ((), ())), preferred_element_type=F32)


def _dot_tn(a, b):
    return lax.dot_general(a, b, (((0,), (0,)), ((), ())), preferred_element_type=F32)


def _split_bf16(v):
    hi = v.astype(BF16)
    lo = (v - hi.astype(F32)).astype(BF16)
    return hi, lo


def _log_sigmoid(z):
    return jnp.minimum(z, 0.0) - jnp.log(1.0 + jnp.exp(-jnp.abs(z)))


IN_SPLITS = (("qg", 256, F32), ("kg", 256, F32), ("vg", 512, BF16), ("gg", 512, F32),
             ("alr", LANES, F32), ("qs", 512, BF16), ("ks", 512, BF16), ("vs", 512, BF16))


def _inproj_kernel(x_ref, w_ref, *out_refs):
    xb = x_ref[...].astype(BF16)
    off = 0
    for (name, width, _), o_ref in zip(IN_SPLITS, out_refs):
        r = _dot(xb, w_ref[:, off:off + width])
        if name in ("qg", "qs"):
            r = r * (GLA_DK ** -0.5)
        o_ref[...] = r.astype(o_ref.dtype)
        off += width


def _inproj(x2d, w_cat, tm):
    T, D = x2d.shape
    wtot = w_cat.shape[1]
    return pl.pallas_call(
        _inproj_kernel,
        grid=(T // tm,),
        in_specs=[pl.BlockSpec((tm, D), lambda i: (i, 0)),
                  pl.BlockSpec((D, wtot), lambda i: (0, 0))],
        out_specs=[pl.BlockSpec((tm, w), lambda i: (i, 0)) for _, w, _ in IN_SPLITS],
        out_shape=[jax.ShapeDtypeStruct((T, w), dt) for _, w, dt in IN_SPLITS],
        compiler_params=pltpu.CompilerParams(dimension_semantics=("arbitrary",),
                                             vmem_limit_bytes=VMEM_LIMIT),
        name="inproj",
    )(x2d, w_cat)


def _gla_kernel(q_ref, k_ref, v_ref, g_ref, a_ref, wg_ref, bg_ref, gn_ref, o_ref, st_ref, *, n_chunks):
    @pl.when(pl.program_id(1) == 0)
    def _():
        st_ref[...] = jnp.zeros_like(st_ref)

    C = GLA_CHUNK
    KW, VW = 2 * GLA_DK, 2 * GLA_DV
    lane = lax.broadcasted_iota(jnp.int32, (1, KW), 1)
    head_mask = [lane < GLA_DK, lane >= GLA_DK]
    r_i = lax.broadcasted_iota(jnp.int32, (C, C), 0)
    c_i = lax.broadcasted_iota(jnp.int32, (C, C), 1)
    causal = c_i <= r_i
    tri = jnp.where(causal, 1.0, 0.0).astype(BF16)

    la_all = _log_sigmoid(_dot(a_ref[...].astype(BF16), wg_ref[...]) + bg_ref[...]) * (1.0 / GLA_GATE_TEMP)

    for c in range(n_chunks):
        rows = slice(c * C, (c + 1) * C)
        la = la_all[rows]
        hi, lo = _split_bf16(la)
        b_all = _dot(tri, hi) + _dot(tri, lo)
        for hp in range(GLA_HEADS // 2):
            kcols = slice(hp * KW, (hp + 1) * KW)
            vcols = slice(hp * VW, (hp + 1) * VW)
            b = b_all[:, kcols]
            b_last = b[C - 1:C, :]
            q = q_ref[rows, kcols]
            k = k_ref[rows, kcols]
            q_in = q * jnp.exp(b)
            k_in = (k * jnp.exp(-b)).astype(BF16)
            k_st = (k * jnp.exp(b_last - b)).astype(BF16)
            dec = jnp.exp(b_last)
            v = v_ref[rows, vcols]
            st = st_ref[hp]
            st_b = st.astype(BF16)
            outs = []
            for h in range(2):
                qm = jnp.where(head_mask[h], q_in, 0.0).astype(BF16)
                att = jnp.where(causal, _dot_nt(qm, k_in), 0.0)
                vh = v[:, h * GLA_DV:(h + 1) * GLA_DV]
                o_h = _dot(att.astype(BF16), vh) + _dot_nt(qm, st_b[h * GLA_DV:(h + 1) * GLA_DV, :])
                ms = jnp.mean(o_h * o_h, axis=-1, keepdims=True)
                outs.append(o_h * lax.rsqrt(ms + EPS))
            st_ref[hp] = st * dec + _dot_tn(v, k_st)
            o = jnp.concatenate(outs, axis=1) * gn_ref[:, vcols]
            g = g_ref[rows, vcols]
            o_ref[rows, vcols] = (o * (g * jax.nn.sigmoid(g))).astype(o_ref.dtype)


def _gla(qg, kg, vg, gg, alr, wg, bg, gn, B, S, ls):
    n_s = S // ls
    kw, vw = GLA_HEADS * GLA_DK, GLA_HEADS * GLA_DV
    row = lambda b, s: (b * n_s + s, 0)
    par = lambda b, s: (0, 0)
    return pl.pallas_call(
        functools.partial(_gla_kernel, n_chunks=ls // GLA_CHUNK),
        grid=(B, n_s),
        in_specs=[pl.BlockSpec((ls, kw), row),
                  pl.BlockSpec((ls, kw), row),
                  pl.BlockSpec((ls, vw), row),
                  pl.BlockSpec((ls, vw), row),
                  pl.BlockSpec((ls, LANES), row),
                  pl.BlockSpec((LANES, kw), par),
                  pl.BlockSpec((1, kw), par),
                  pl.BlockSpec((1, vw), par)],
        out_specs=pl.BlockSpec((ls, vw), row),
        out_shape=jax.ShapeDtypeStruct((B * S, vw), BF16),
        scratch_shapes=[pltpu.VMEM((GLA_HEADS // 2, 2 * GLA_DV, 2 * GLA_DK), F32)],
        compiler_params=pltpu.CompilerParams(dimension_semantics=("arbitrary", "arbitrary"),
                                             vmem_limit_bytes=VMEM_LIMIT),
        name="gla",
    )(qg, kg, vg, gg, alr, wg, bg, gn)


SB_T = 128
SB_GW = 2 * LANES
SB_GROUPS = SB_HEADS * SB_DH // SB_GW
SB_TILES = 4 * SB_GROUPS


def _sb_kernel(q_ref, k_ref, v_ref, tu_ref, gn_ref, o_ref,
               acc_ref, car_ref, kb_ref, vb_ref, zl_ref, l_ref, cs_ref):
    i = pl.program_id(1)
    T = SB_T
    lane = lax.broadcasted_iota(jnp.int32, (1, SB_GW), 1)
    is_h0 = (lane % LANES) < SB_DH
    qg = []
    for g in range(SB_GROUPS):
        q = q_ref[:, g * SB_GW:(g + 1) * SB_GW]
        zero = jnp.zeros_like(q)
        qg.append(jnp.concatenate([jnp.where(is_h0, q, zero), jnp.where(is_h0, zero, q)], axis=0))
    row = lax.broadcasted_iota(jnp.int32, (T, T), 0)
    col = lax.broadcasted_iota(jnp.int32, (T, T), 1)
    acc_ref[...] = jnp.zeros_like(acc_ref)
    car_ref[...] = jnp.zeros_like(car_ref)
    kb_ref[...] = jnp.zeros_like(kb_ref)
    vb_ref[...] = jnp.zeros_like(vb_ref)

    def sweep(j, diagonal):
        ks = pl.multiple_of(j * T, T)
        valid = col < row
        for g in range(SB_GROUPS):
            for p in range(2):
                src = slice(g * SB_GW + p * LANES, g * SB_GW + (p + 1) * LANES)
                kb_ref[g, p * T:(p + 1) * T, p * LANES:(p + 1) * LANES] = k_ref[pl.ds(ks, T), src]
                vb_ref[g, p * T:(p + 1) * T, p * LANES:(p + 1) * LANES] = v_ref[pl.ds(ks, T), src]
            z = _dot_nt(qg[g], kb_ref[g])
            for h in range(2):
                for p in range(2):
                    n = 4 * g + 2 * h + p
                    zq = z[h * T:(h + 1) * T, p * T:(p + 1) * T]
                    lnb = -(jnp.maximum(zq, 0.0) + jnp.log(1.0 + jnp.exp(-jnp.abs(zq))))
                    if diagonal:
                        lnb = jnp.where(valid, lnb, 0.0)
                    hi, lo = _split_bf16(lnb)
                    l_ref[n * T:(n + 1) * T, :T] = hi
                    l_ref[n * T:(n + 1) * T, T:] = lo
                    zl_ref[n] = zq + lnb
        cs_ref[...] = _dot(l_ref[...], tu_ref[...])
        top = None
        for g in range(SB_GROUPS):
            w_rows = []
            for h in range(2):
                w_cols = []
                for p in range(2):
                    n = 4 * g + 2 * h + p
                    car = car_ref[n]
                    w = jnp.exp(zl_ref[n] + cs_ref[n * T:(n + 1) * T, :T] + car)
                    if diagonal:
                        w = jnp.where(valid, w, 0.0)
                    w_cols.append(w.astype(BF16))
                    car = car + cs_ref[n * T:(n + 1) * T, T:]
                    car_ref[n] = car
                    top = car if top is None else jnp.maximum(top, car)
                w_rows.append(jnp.concatenate(w_cols, axis=1))
            acc_ref[g] += _dot(jnp.concatenate(w_rows, axis=0), vb_ref[g])
        return jnp.max(top)

    def cond(carry):
        j, alive = carry
        return jnp.logical_and(j >= 0, alive > SB_DEAD)

    def body(carry):
        j, _ = carry
        return j - 1, sweep(j, diagonal=False)

    lax.while_loop(cond, body, (i - 1, sweep(i, diagonal=True)))

    head_of_lane = lane // SB_DH
    for g in range(SB_GROUPS):
        a = acc_ref[g]
        o = jnp.where(is_h0, a[:T], a[T:])
        sq = o * o
        ms = jnp.zeros_like(o)
        for hd in range(SB_GW // SB_DH):
            mine = head_of_lane == hd
            ms = jnp.where(mine, jnp.sum(jnp.where(mine, sq, 0.0), axis=-1, keepdims=True), ms)
        cols = slice(g * SB_GW, (g + 1) * SB_GW)
        o_ref[:, cols] = (o * lax.rsqrt(ms * (1.0 / SB_DH) + EPS) * gn_ref[:, cols]).astype(o_ref.dtype)


def _sb(qs, ks, vs, tu, gn, B, S):
    n_q = S // SB_T
    width = SB_HEADS * SB_DH
    return pl.pallas_call(
        _sb_kernel,
        grid=(B, n_q),
        in_specs=[pl.BlockSpec((SB_T, width), lambda b, i: (b * n_q + i, 0)),
                  pl.BlockSpec((S, width), lambda b, i: (b, 0)),
                  pl.BlockSpec((S, width), lambda b, i: (b, 0)),
                  pl.BlockSpec((2 * SB_T, 2 * SB_T), lambda b, i: (0, 0)),
                  pl.BlockSpec((1, width), lambda b, i: (0, 0))],
        out_specs=pl.BlockSpec((SB_T, width), lambda b, i: (b * n_q + i, 0)),
        out_shape=jax.ShapeDtypeStruct((B * S, width), BF16),
        scratch_shapes=[pltpu.VMEM((SB_GROUPS, 2 * SB_T, SB_GW), F32),
                        pltpu.VMEM((SB_TILES, SB_T, SB_T), F32),
                        pltpu.VMEM((SB_GROUPS, 2 * SB_T, SB_GW), BF16),
                        pltpu.VMEM((SB_GROUPS, 2 * SB_T, SB_GW), BF16),
                        pltpu.VMEM((SB_TILES, SB_T, SB_T), F32),
                        pltpu.VMEM((SB_TILES * SB_T, 2 * SB_T), BF16),
                        pltpu.VMEM((SB_TILES * SB_T, 2 * SB_T), F32)],
        compiler_params=pltpu.CompilerParams(dimension_semantics=("arbitrary", "arbitrary"),
                                             vmem_limit_bytes=VMEM_LIMIT),
        name="sb",
    )(qs, ks, vs, tu, gn)


def _layer_norm(y, g, b):
    mu = jnp.mean(y, axis=-1, keepdims=True)
    d = y - mu
    var = jnp.mean(d * d, axis=-1, keepdims=True)
    return d * lax.rsqrt(var + EPS) * g + b


def _outproj_kernel(og_ref, os_ref, x_ref, wo_ref, g_ref, b_ref, x1_ref, x1b_ref):
    half = og_ref.shape[1]
    mix = _dot(og_ref[...], wo_ref[:half, :]) + _dot(os_ref[...], wo_ref[half:, :])
    x1 = _layer_norm(ALPHA * x_ref[...] + mix, g_ref[...], b_ref[...])
    x1_ref[...] = x1
    x1b_ref[...] = x1.astype(BF16)


def _outproj(ogla, osb, x2d, wo, g, b, tm):
    T, D = x2d.shape
    half = ogla.shape[1]
    return pl.pallas_call(
        _outproj_kernel,
        grid=(T // tm,),
        in_specs=[pl.BlockSpec((tm, half), lambda i: (i, 0)),
                  pl.BlockSpec((tm, half), lambda i: (i, 0)),
                  pl.BlockSpec((tm, D), lambda i: (i, 0)),
                  pl.BlockSpec((2 * half, D), lambda i: (0, 0)),
                  pl.BlockSpec((1, D), lambda i: (0, 0)),
                  pl.BlockSpec((1, D), lambda i: (0, 0))],
        out_specs=[pl.BlockSpec((tm, D), lambda i: (i, 0)),
                   pl.BlockSpec((tm, D), lambda i: (i, 0))],
        out_shape=[jax.ShapeDtypeStruct((T, D), F32), jax.ShapeDtypeStruct((T, D), BF16)],
        compiler_params=pltpu.CompilerParams(dimension_semantics=("arbitrary",),
                                             vmem_limit_bytes=VMEM_LIMIT),
        name="outproj",
    )(ogla, osb, x2d, wo, g, b)


SUBLANES = 8


def _batcher_network(n):
    pairs, p = [], 1
    while p < n:
        k = p
        while k >= 1:
            for j in range(k % p, n - k, 2 * k):
                for i in range(min(k, n - j - k)):
                    if (i + j) // (2 * p) == (i + j + k) // (2 * p):
                        pairs.append((i + j, i + j + k))
            k //= 2
        p *= 2
    return pairs


def _merge_top(stacks, n):
    stacks = [list(st) for st in stacks]
    tops = []
    for r in range(n):
        head = stacks[0][0]
        for st in stacks[1:]:
            head = jnp.maximum(head, st[0])
        m = jnp.max(head, axis=0, keepdims=True)
        tops.append(m)
        need = n - r - 1
        for st in stacks:
            if need == 0:
                break
            hit = st[0] == m
            depth = min(need, len(st))
            for d in range(depth):
                below = st[d + 1] if d + 1 < len(st) else NEG_INF
                st[d] = jnp.where(hit, below, st[d])
            del st[depth:]
    return tops


def _top_rows(s, n):
    slabs = [s[r:r + SUBLANES] for r in range(0, s.shape[0], SUBLANES)]
    for i, j in _batcher_network(len(slabs)):
        slabs[i], slabs[j] = jnp.maximum(slabs[i], slabs[j]), jnp.minimum(slabs[i], slabs[j])
    return _merge_top([slabs], n)


def _peer_sel_kernel(x_ref, wq_ref, sk_ref, s1_ref, e1_ref, th_ref, c_ref, q_scr):
    tt = x_ref.shape[0]
    qry = _dot(x_ref[...], wq_ref[...]).astype(BF16)
    for hp in range(2 * PEER_HEADS):
        q_scr[hp] = qry[:, hp * PEER_HALF:(hp + 1) * PEER_HALF]
    sub = lax.broadcasted_iota(jnp.int32, (SUBLANES, tt), 0)

    def head(h):
        s0 = _dot_nt(sk_ref[2 * h], q_scr[2 * h])
        s1 = _dot_nt(sk_ref[2 * h + 1], q_scr[2 * h + 1])
        n = PEER_TOPK + 1
        a = _top_rows(s0, n)
        b = _top_rows(s1, n)
        a_lo = jnp.concatenate(a[:SUBLANES], axis=0)
        a_hi = jnp.concatenate(a[SUBLANES:2 * SUBLANES], axis=0)
        xs = [jnp.where(sub < n // l, a_lo + b[l - 1], NEG_INF) for l in range(1, n + 1)]
        zs = [a_hi + b[0], jnp.where(sub < 1, a[n - 1] + b[0], NEG_INF)]
        best = _merge_top([xs, zs], n)
        z = jnp.zeros_like(best[0])
        for r in range(PEER_TOPK):
            z = z + jnp.exp(best[r] - best[0])
        tau = 0.5 * (best[PEER_TOPK - 1] + best[PEER_TOPK])
        d1 = s1 - b[0]
        d0 = s0 - a[0]
        tau_rel = tau - best[0]
        scale = PEER_QONE / jnp.maximum(-tau_rel, 1e-30)
        s1_ref[0, h] = jnp.clip(d1 * scale, -2.0 * PEER_QONE, 0.0).astype(jnp.int32).astype(jnp.int16)
        th_ref[0, h] = jnp.clip((tau_rel - d0) * scale, -2.0 * PEER_QONE, 2.0 * PEER_QONE - 1.0).astype(jnp.int32)
        e1_ref[0, h] = jnp.exp(d1).astype(BF16)
        c_ref[0, h] = (0.5 * jnp.exp(d0)) / z

    def head_pair(i, _):
        head(2 * i)
        head(2 * i + 1)
        return 0

    lax.fori_loop(0, PEER_HEADS // 2, head_pair, 0)


def _peer_sel(x1b, wq, sk, tt):
    T, D = x1b.shape
    qd = wq.shape[1]
    shp = [jax.ShapeDtypeStruct((T // tt, PEER_HEADS, PEER_NKEYS, tt), dt)
           for dt in (jnp.int16, BF16, jnp.int32, F32)]
    ospec = pl.BlockSpec((1, PEER_HEADS, PEER_NKEYS, tt), lambda i: (i, 0, 0, 0))
    return pl.pallas_call(
        _peer_sel_kernel,
        grid=(T // tt,),
        in_specs=[pl.BlockSpec((tt, D), lambda i: (i, 0)),
                  pl.BlockSpec((D, qd), lambda i: (0, 0)),
                  pl.BlockSpec((2 * PEER_HEADS, PEER_NKEYS, PEER_HALF), lambda i: (0, 0, 0))],
        out_specs=[ospec, ospec, ospec, ospec],
        out_shape=shp,
        scratch_shapes=[pltpu.VMEM((2 * PEER_HEADS, tt, PEER_HALF), BF16)],
        compiler_params=pltpu.CompilerParams(dimension_semantics=("arbitrary",),
                                             vmem_limit_bytes=VMEM_LIMIT),
        name="peer_sel",
    )(x1b, wq, sk)


PEER_ET = 2048
PEER_EQ = 512
PEER_LC = 256


def _peer_ffn_kernel(xt_ref, x1_ref, u_ref, vt_ref, s1_ref, e1_ref, th_ref, c_ref, g_ref, b_ref,
                     o_ref, acc_ref, h_ref, p_ref):
    e = pl.program_id(1)
    n_lc = xt_ref.shape[0]
    n_q = PEER_ET // PEER_EQ
    n_k = PEER_EQ // PEER_NKEYS
    assert n_lc == 2

    @pl.when(e == 0)
    def _():
        acc_ref[...] = jnp.zeros_like(acc_ref)

    def scores(q, lc):
        r0 = q * PEER_EQ if isinstance(q, int) else pl.multiple_of(q * PEER_EQ, PEER_EQ)
        h_ref[lc] = _dot(u_ref[pl.ds(r0, PEER_EQ), :], xt_ref[lc])

    def gated_act(q, lc):
        for k in range(n_k):
            rows = slice(k * PEER_NKEYS, (k + 1) * PEER_NKEYS)
            gate = jnp.zeros((PEER_NKEYS, PEER_LC), BF16)
            for h in range(PEER_HEADS):
                th = th_ref[lc, h, pl.ds(q * n_k + k, 1), :].astype(jnp.int16)
                cc = c_ref[lc, h, pl.ds(q * n_k + k, 1), :].astype(BF16)
                val = e1_ref[lc, h] * cc
                gate = gate + jnp.where(s1_ref[lc, h] >= th, val, jnp.zeros_like(val))
            hh = h_ref[lc, rows, :]
            act = hh + hh * lax.erf(hh * (2.0 ** -0.5))
            p_ref[lc, rows, :] = gate * act.astype(BF16)

    def accumulate(q, lc):
        acc_ref[lc] += _dot(vt_ref[q], p_ref[lc])

    scores(0, 0)
    scores(0, 1)
    gated_act(0, 0)

    def step(q, _):
        scores(q + 1, 0)
        gated_act(q, 1)
        accumulate(q, 0)
        scores(q + 1, 1)
        gated_act(q + 1, 0)
        accumulate(q, 1)
        return 0

    lax.fori_loop(0, n_q - 1, step, 0)
    gated_act(n_q - 1, 1)
    accumulate(n_q - 1, 0)
    accumulate(n_q - 1, 1)

    @pl.when(e == pl.num_programs(1) - 1)
    def _():
        for lc in range(n_lc):
            rows = slice(lc * PEER_LC, (lc + 1) * PEER_LC)
            y = ALPHA * x1_ref[rows, :] + acc_ref[lc].T
            o_ref[rows, :] = _layer_norm(y, g_ref[...], b_ref[...])


def _peer_ffn(xt, x1, u, vt, s1, e1, th, cc, g, b, tt):
    T, D = x1.shape
    E = u.shape[0]
    n_lc = tt // PEER_LC
    n_q = PEER_ET // PEER_EQ
    tok = pl.BlockSpec((n_lc, PEER_HEADS, PEER_NKEYS, PEER_LC), lambda t, e: (t, 0, 0, 0))
    sel = pl.BlockSpec((n_lc, PEER_HEADS, PEER_ET // PEER_NKEYS, PEER_LC), lambda t, e: (t, 0, e, 0))
    return pl.pallas_call(
        _peer_ffn_kernel,
        grid=(T // tt, E // PEER_ET),
        in_specs=[pl.BlockSpec((n_lc, D, PEER_LC), lambda t, e: (t, 0, 0)),
                  pl.BlockSpec((tt, D), lambda t, e: (t, 0)),
                  pl.BlockSpec((PEER_ET, D), lambda t, e: (e, 0)),
                  pl.BlockSpec((n_q, D, PEER_EQ), lambda t, e: (e, 0, 0)),
                  tok, tok, sel, sel,
                  pl.BlockSpec((1, D), lambda t, e: (0, 0)),
                  pl.BlockSpec((1, D), lambda t, e: (0, 0))],
        out_specs=pl.BlockSpec((tt, D), lambda t, e: (t, 0)),
        out_shape=jax.ShapeDtypeStruct((T, D), F32),
        scratch_shapes=[pltpu.VMEM((n_lc, D, PEER_LC), F32),
                        pltpu.VMEM((n_lc, PEER_EQ, PEER_LC), F32),
                        pltpu.VMEM((n_lc, PEER_EQ, PEER_LC), BF16)],
        compiler_params=pltpu.CompilerParams(dimension_semantics=("arbitrary", "arbitrary"),
                                             vmem_limit_bytes=VMEM_LIMIT),
        name="peer_ffn",
    )(xt, x1, u, vt, s1, e1, th, cc, g, b)


def _layer(x, w_in, w_gla_gate, b_gla_gate, gla_norm_g, sb_norm_g, w_out, ln1_g, ln1_b,
           peer_w_query, peer_sub_keys, peer_u, peer_v, ln2_g, ln2_b):
    B, S, D = x.shape
    T = B * S
    x2d = x.reshape(T, D)

    kw, gw, sw = GLA_HEADS * GLA_DK, GLA_HEADS * GLA_DV, SB_HEADS * SB_DH
    pts = [0, kw, 2 * kw, 2 * kw + gw, 2 * kw + 2 * gw, 2 * kw + 2 * gw + GLA_RANK]
    pts += [pts[-1] + sw, pts[-1] + 2 * sw, pts[-1] + 3 * sw]
    cols = [w_in[:, pts[n]:pts[n + 1]] for n in range(8)]
    cols[4] = jnp.pad(cols[4], ((0, 0), (0, LANES - GLA_RANK)))
    w_cat = jnp.concatenate(cols, axis=1).astype(BF16)
    wg = jnp.pad(w_gla_gate, ((0, LANES - GLA_RANK), (0, 0))).astype(BF16)

    qg, kg, vg, gg, alr, qs, ks, vs = _inproj(x2d, w_cat, min(512, T))

    ogla = _gla(qg, kg, vg, gg, alr, wg, b_gla_gate.reshape(1, -1), gla_norm_g.reshape(1, -1),
                B, S, min(512, S))

    r = lax.broadcasted_iota(jnp.int32, (2 * SB_T, 2 * SB_T), 0) % SB_T
    c = lax.broadcasted_iota(jnp.int32, (2 * SB_T, 2 * SB_T), 1)
    tu = jnp.where((c >= SB_T) | (r > c), 1.0, 0.0).astype(BF16)
    osb = _sb(qs, ks, vs, tu, sb_norm_g.reshape(1, -1), B, S)

    x1, x1b = _outproj(ogla, osb, x2d, w_out.astype(BF16), ln1_g.reshape(1, D), ln1_b.reshape(1, D),
                       min(512, T))

    sk = peer_sub_keys.reshape(2 * PEER_HEADS, PEER_NKEYS, PEER_HALF).astype(BF16)
    s1, e1, th, cc = _peer_sel(x1b, peer_w_query.astype(BF16), sk, PEER_LC)

    xt = x1b.reshape(-1, PEER_LC, D).transpose(0, 2, 1)
    vt = peer_v.reshape(-1, PEER_EQ, D).transpose(0, 2, 1).astype(BF16)
    out = _peer_ffn(xt, x1, peer_u.astype(BF16), vt, s1, e1, th, cc,
                    ln2_g.reshape(1, D), ln2_b.reshape(1, D), 2 * PEER_LC)
    return out.reshape(B, S, D)


def kernel(x, w_in, w_gla_gate, b_gla_gate, gla_norm_g, sb_norm_g, w_out, ln1_g, ln1_b,
           peer_w_query, peer_sub_keys, peer_u, peer_v, ln2_g, ln2_b):
    for l in range(DEPTH):
        x = _layer(x, w_in[l], w_gla_gate[l], b_gla_gate[l], gla_norm_g[l], sb_norm_g[l], w_out[l],
                   ln1_g[l], ln1_b[l], peer_w_query[l], peer_sub_keys[l], peer_u[l], peer_v[l],
                   ln2_g[l], ln2_b[l])
    return x
```

```python
import functools

import jax
import jax.numpy as jnp
from jax import lax
from jax.experimental import pallas as pl
from jax.experimental.pallas import tpu as pltpu

F32 = jnp.float32
BF16 = jnp.bfloat16

LANES = 128
GLA_HEADS = 4
GLA_DK = 64
GLA_DV = 128
GLA_RANK = 16
GLA_GATE_TEMP = 16.0
GLA_CHUNK = 64
SB_HEADS = 8
SB_DH = 64
PEER_HEADS = 8
PEER_NKEYS = 128
PEER_HALF = 128
PEER_TOPK = 16
PEER_QONE = 16384.0
DEPTH = 1
ALPHA = (2.0 * DEPTH) ** 0.25
EPS = 1e-5
NEG_INF = float("-inf")
SB_DEAD = -104.0

VMEM_CAPACITY_V7X = 64 * 1024 * 1024
VMEM_LIMIT = VMEM_CAPACITY_V7X * 7 // 8
TOKEN_TILE = 512
GLA_SEQ_BLOCK = 512


def _dot(a, b):
    return jnp.dot(a, b, preferred_element_type=F32)


def _dot_nt(a, b):
    return lax.dot_general(a, b, (((1,), (1,)), ((), ())), preferred_element_type=F32)


def _dot_tn(a, b):
    return lax.dot_general(a, b, (((0,), (0,)), ((), ())), preferred_element_type=F32)


def _split_bf16(v):
    hi = v.astype(BF16)
    lo = (v - hi.astype(F32)).astype(BF16)
    return hi, lo


def _log_sigmoid(z):
    return jnp.minimum(z, 0.0) - jnp.log(1.0 + jnp.exp(-jnp.abs(z)))


IN_SPLITS = (("qg", 256, F32), ("kg", 256, F32), ("vg", 512, BF16), ("gg", 512, F32),
             ("alr", LANES, F32), ("qs", 512, BF16), ("ks", 512, BF16), ("vs", 512, BF16))


def _inproj_kernel(x_ref, w_ref, *out_refs):
    xb = x_ref[...].astype(BF16)
    off = 0
    for (name, width, _), o_ref in zip(IN_SPLITS, out_refs):
        r = _dot(xb, w_ref[:, off:off + width])
        if name in ("qg", "qs"):
            r = r * (GLA_DK ** -0.5)
        o_ref[...] = r.astype(o_ref.dtype)
        off += width


def _inproj(x2d, w_cat, tm):
    T, D = x2d.shape
    wtot = w_cat.shape[1]
    return pl.pallas_call(
        _inproj_kernel,
        grid=(T // tm,),
        in_specs=[pl.BlockSpec((tm, D), lambda i: (i, 0)),
                  pl.BlockSpec((D, wtot), lambda i: (0, 0))],
        out_specs=[pl.BlockSpec((tm, w), lambda i: (i, 0)) for _, w, _ in IN_SPLITS],
        out_shape=[jax.ShapeDtypeStruct((T, w), dt) for _, w, dt in IN_SPLITS],
        compiler_params=pltpu.CompilerParams(dimension_semantics=("arbitrary",),
                                             vmem_limit_bytes=VMEM_LIMIT),
        name="inproj",
    )(x2d, w_cat)


def _gla_kernel(q_ref, k_ref, v_ref, g_ref, a_ref, wg_ref, bg_ref, gn_ref, o_ref, st_ref, *, n_chunks):
    @pl.when(pl.program_id(1) == 0)
    def _():
        st_ref[...] = jnp.zeros_like(st_ref)

    C = GLA_CHUNK
    KW, VW = 2 * GLA_DK, 2 * GLA_DV
    lane = lax.broadcasted_iota(jnp.int32, (1, KW), 1)
    head_mask = [lane < GLA_DK, lane >= GLA_DK]
    r_i = lax.broadcasted_iota(jnp.int32, (C, C), 0)
    c_i = lax.broadcasted_iota(jnp.int32, (C, C), 1)
    causal = c_i <= r_i
    tri = jnp.where(causal, 1.0, 0.0).astype(BF16)

    la_all = _log_sigmoid(_dot(a_ref[...].astype(BF16), wg_ref[...]) + bg_ref[...]) * (1.0 / GLA_GATE_TEMP)

    for c in range(n_chunks):
        rows = slice(c * C, (c + 1) * C)
        la = la_all[rows]
        hi, lo = _split_bf16(la)
        b_all = _dot(tri, hi) + _dot(tri, lo)
        for hp in range(GLA_HEADS // 2):
            kcols = slice(hp * KW, (hp + 1) * KW)
            vcols = slice(hp * VW, (hp + 1) * VW)
            b = b_all[:, kcols]
            b_last = b[C - 1:C, :]
            q = q_ref[rows, kcols]
            k = k_ref[rows, kcols]
            q_in = q * jnp.exp(b)
            k_in = (k * jnp.exp(-b)).astype(BF16)
            k_st = (k * jnp.exp(b_last - b)).astype(BF16)
            dec = jnp.exp(b_last)
            v = v_ref[rows, vcols]
            st = st_ref[hp]
            st_b = st.astype(BF16)
            outs = []
            for h in range(2):
                qm = jnp.where(head_mask[h], q_in, 0.0).astype(BF16)
                att = jnp.where(causal, _dot_nt(qm, k_in), 0.0)
                vh = v[:, h * GLA_DV:(h + 1) * GLA_DV]
                o_h = _dot(att.astype(BF16), vh) + _dot_nt(qm, st_b[h * GLA_DV:(h + 1) * GLA_DV, :])
                ms = jnp.mean(o_h * o_h, axis=-1, keepdims=True)
                outs.append(o_h * lax.rsqrt(ms + EPS))
            st_ref[hp] = st * dec + _dot_tn(v, k_st)
            o = jnp.concatenate(outs, axis=1) * gn_ref[:, vcols]
            g = g_ref[rows, vcols]
            o_ref[rows, vcols] = (o * (g * jax.nn.sigmoid(g))).astype(o_ref.dtype)


def _gla(qg, kg, vg, gg, alr, wg, bg, gn, B, S, ls):
    n_s = S // ls
    kw, vw = GLA_HEADS * GLA_DK, GLA_HEADS * GLA_DV
    row = lambda b, s: (b * n_s + s, 0)
    par = lambda b, s: (0, 0)
    return pl.pallas_call(
        functools.partial(_gla_kernel, n_chunks=ls // GLA_CHUNK),
        grid=(B, n_s),
        in_specs=[pl.BlockSpec((ls, kw), row),
                  pl.BlockSpec((ls, kw), row),
                  pl.BlockSpec((ls, vw), row),
                  pl.BlockSpec((ls, vw), row),
                  pl.BlockSpec((ls, LANES), row),
                  pl.BlockSpec((LANES, kw), par),
                  pl.BlockSpec((1, kw), par),
                  pl.BlockSpec((1, vw), par)],
        out_specs=pl.BlockSpec((ls, vw), row),
        out_shape=jax.ShapeDtypeStruct((B * S, vw), BF16),
        scratch_shapes=[pltpu.VMEM((GLA_HEADS // 2, 2 * GLA_DV, 2 * GLA_DK), F32)],
        compiler_params=pltpu.CompilerParams(dimension_semantics=("arbitrary", "arbitrary"),
                                             vmem_limit_bytes=VMEM_LIMIT),
        name="gla",
    )(qg, kg, vg, gg, alr, wg, bg, gn)


SB_T = 128
SB_GW = 2 * LANES
SB_GROUPS = SB_HEADS * SB_DH // SB_GW
SB_TILES = 4 * SB_GROUPS


def _sb_kernel(q_ref, k_ref, v_ref, tu_ref, gn_ref, o_ref,
               acc_ref, car_ref, kb_ref, vb_ref, zl_ref, l_ref, cs_ref):
    i = pl.program_id(1)
    T = SB_T
    lane = lax.broadcasted_iota(jnp.int32, (1, SB_GW), 1)
    is_h0 = (lane % LANES) < SB_DH
    qg = []
    for g in range(SB_GROUPS):
        q = q_ref[:, g * SB_GW:(g + 1) * SB_GW]
        zero = jnp.zeros_like(q)
        qg.append(jnp.concatenate([jnp.where(is_h0, q, zero), jnp.where(is_h0, zero, q)], axis=0))
    row = lax.broadcasted_iota(jnp.int32, (T, T), 0)
    col = lax.broadcasted_iota(jnp.int32, (T, T), 1)
    acc_ref[...] = jnp.zeros_like(acc_ref)
    car_ref[...] = jnp.zeros_like(car_ref)
    kb_ref[...] = jnp.zeros_like(kb_ref)
    vb_ref[...] = jnp.zeros_like(vb_ref)

    def sweep(j, diagonal):
        ks = pl.multiple_of(j * T, T)
        valid = col < row
        for g in range(SB_GROUPS):
            for p in range(2):
                src = slice(g * SB_GW + p * LANES, g * SB_GW + (p + 1) * LANES)
                kb_ref[g, p * T:(p + 1) * T, p * LANES:(p + 1) * LANES] = k_ref[pl.ds(ks, T), src]
                vb_ref[g, p * T:(p + 1) * T, p * LANES:(p + 1) * LANES] = v_ref[pl.ds(ks, T), src]
            z = _dot_nt(qg[g], kb_ref[g])
            for h in range(2):
                for p in range(2):
                    n = 4 * g + 2 * h + p
                    zq = z[h * T:(h + 1) * T, p * T:(p + 1) * T]
                    lnb = -(jnp.maximum(zq, 0.0) + jnp.log(1.0 + jnp.exp(-jnp.abs(zq))))
                    if diagonal:
                        lnb = jnp.where(valid, lnb, 0.0)
                    hi, lo = _split_bf16(lnb)
                    l_ref[n * T:(n + 1) * T, :T] = hi
                    l_ref[n * T:(n + 1) * T, T:] = lo
                    zl_ref[n] = zq + lnb
        cs_ref[...] = _dot(l_ref[...], tu_ref[...])
        top = None
        for g in range(SB_GROUPS):
            w_rows = []
            for h in range(2):
                w_cols = []
                for p in range(2):
                    n = 4 * g + 2 * h + p
                    car = car_ref[n]
                    w = jnp.exp(zl_ref[n] + cs_ref[n * T:(n + 1) * T, :T] + car)
                    if diagonal:
                        w = jnp.where(valid, w, 0.0)
                    w_cols.append(w.astype(BF16))
                    car = car + cs_ref[n * T:(n + 1) * T, T:]
                    car_ref[n] = car
                    top = car if top is None else jnp.maximum(top, car)
                w_rows.append(jnp.concatenate(w_cols, axis=1))
            acc_ref[g] += _dot(jnp.concatenate(w_rows, axis=0), vb_ref[g])
        return jnp.max(top)

    def cond(carry):
        j, alive = carry
        return jnp.logical_and(j >= 0, alive > SB_DEAD)

    def body(carry):
        j, _ = carry
        return j - 1, sweep(j, diagonal=False)

    lax.while_loop(cond, body, (i - 1, sweep(i, diagonal=True)))

    head_of_lane = lane // SB_DH
    for g in range(SB_GROUPS):
        a = acc_ref[g]
        o = jnp.where(is_h0, a[:T], a[T:])
        sq = o * o
        ms = jnp.zeros_like(o)
        for hd in range(SB_GW // SB_DH):
            mine = head_of_lane == hd
            ms = jnp.where(mine, jnp.sum(jnp.where(mine, sq, 0.0), axis=-1, keepdims=True), ms)
        cols = slice(g * SB_GW, (g + 1) * SB_GW)
        o_ref[:, cols] = (o * lax.rsqrt(ms * (1.0 / SB_DH) + EPS) * gn_ref[:, cols]).astype(o_ref.dtype)


def _sb(qs, ks, vs, tu, gn, B, S):
    n_q = S // SB_T
    width = SB_HEADS * SB_DH
    return pl.pallas_call(
        _sb_kernel,
        grid=(B, n_q),
        in_specs=[pl.BlockSpec((SB_T, width), lambda b, i: (b * n_q + i, 0)),
                  pl.BlockSpec((S, width), lambda b, i: (b, 0)),
                  pl.BlockSpec((S, width), lambda b, i: (b, 0)),
                  pl.BlockSpec((2 * SB_T, 2 * SB_T), lambda b, i: (0, 0)),
                  pl.BlockSpec((1, width), lambda b, i: (0, 0))],
        out_specs=pl.BlockSpec((SB_T, width), lambda b, i: (b * n_q + i, 0)),
        out_shape=jax.ShapeDtypeStruct((B * S, width), BF16),
        scratch_shapes=[pltpu.VMEM((SB_GROUPS, 2 * SB_T, SB_GW), F32),
                        pltpu.VMEM((SB_TILES, SB_T, SB_T), F32),
                        pltpu.VMEM((SB_GROUPS, 2 * SB_T, SB_GW), BF16),
                        pltpu.VMEM((SB_GROUPS, 2 * SB_T, SB_GW), BF16),
                        pltpu.VMEM((SB_TILES, SB_T, SB_T), F32),
                        pltpu.VMEM((SB_TILES * SB_T, 2 * SB_T), BF16),
                        pltpu.VMEM((SB_TILES * SB_T, 2 * SB_T), F32)],
        compiler_params=pltpu.CompilerParams(dimension_semantics=("arbitrary", "arbitrary"),
                                             vmem_limit_bytes=VMEM_LIMIT),
        name="sb",
    )(qs, ks, vs, tu, gn)


def _layer_norm(y, g, b):
    mu = jnp.mean(y, axis=-1, keepdims=True)
    d = y - mu
    var = jnp.mean(d * d, axis=-1, keepdims=True)
    return d * lax.rsqrt(var + EPS) * g + b


def _outproj_kernel(og_ref, os_ref, x_ref, wo_ref, g_ref, b_ref, x1_ref, x1b_ref):
    half = og_ref.shape[1]
    mix = _dot(og_ref[...], wo_ref[:half, :]) + _dot(os_ref[...], wo_ref[half:, :])
    x1 = _layer_norm(ALPHA * x_ref[...] + mix, g_ref[...], b_ref[...])
    x1_ref[...] = x1
    x1b_ref[...] = x1.astype(BF16)


def _outproj(ogla, osb, x2d, wo, g, b, tm):
    T, D = x2d.shape
    half = ogla.shape[1]
    return pl.pallas_call(
        _outproj_kernel,
        grid=(T // tm,),
        in_specs=[pl.BlockSpec((tm, half), lambda i: (i, 0)),
                  pl.BlockSpec((tm, half), lambda i: (i, 0)),
                  pl.BlockSpec((tm, D), lambda i: (i, 0)),
                  pl.BlockSpec((2 * half, D), lambda i: (0, 0)),
                  pl.BlockSpec((1, D), lambda i: (0, 0)),
                  pl.BlockSpec((1, D), lambda i: (0, 0))],
        out_specs=[pl.BlockSpec((tm, D), lambda i: (i, 0)),
                   pl.BlockSpec((tm, D), lambda i: (i, 0))],
        out_shape=[jax.ShapeDtypeStruct((T, D), F32), jax.ShapeDtypeStruct((T, D), BF16)],
        compiler_params=pltpu.CompilerParams(dimension_semantics=("arbitrary",),
                                             vmem_limit_bytes=VMEM_LIMIT),
        name="outproj",
    )(ogla, osb, x2d, wo, g, b)


SUBLANES = 8
PACKED_ROWS = 16


def _batcher_network(n):
    pairs, p = [], 1
    while p < n:
        k = p
        while k >= 1:
            for j in range(k % p, n - k, 2 * k):
                for i in range(min(k, n - j - k)):
                    if (i + j) // (2 * p) == (i + j + k) // (2 * p):
                        pairs.append((i + j, i + j + k))
            k //= 2
        p *= 2
    return pairs


def _merge_top(stacks, n):
    stacks = [list(st) for st in stacks]
    tops = []
    for r in range(n):
        head = stacks[0][0]
        for st in stacks[1:]:
            head = jnp.maximum(head, st[0])
        m = jnp.max(head, axis=0, keepdims=True)
        tops.append(m)
        need = n - r - 1
        for st in stacks:
            if need == 0:
                break
            hit = st[0] == m
            depth = min(need, len(st))
            for d in range(depth):
                below = st[d + 1] if d + 1 < len(st) else NEG_INF
                st[d] = jnp.where(hit, below, st[d])
            del st[depth:]
    return tops


def _top_rows(s, n):
    slabs = [s[r:r + SUBLANES] for r in range(0, s.shape[0], SUBLANES)]
    for i, j in _batcher_network(len(slabs)):
        slabs[i], slabs[j] = jnp.maximum(slabs[i], slabs[j]), jnp.minimum(slabs[i], slabs[j])
    return _merge_top([slabs], n)


def _peer_sel_kernel(x_ref, wq_ref, sk_ref, s1_ref, e1_ref, th_ref, c_ref, q_scr):
    tt = x_ref.shape[0]
    qry = _dot(x_ref[...], wq_ref[...]).astype(BF16)
    for hp in range(2 * PEER_HEADS):
        q_scr[hp] = qry[:, hp * PEER_HALF:(hp + 1) * PEER_HALF]
    sub = lax.broadcasted_iota(jnp.int32, (SUBLANES, tt), 0)

    def head(h):
        s0 = _dot_nt(sk_ref[2 * h], q_scr[2 * h])
        s1 = _dot_nt(sk_ref[2 * h + 1], q_scr[2 * h + 1])
        n = PEER_TOPK + 1
        a = _top_rows(s0, n)
        b = _top_rows(s1, n)
        a_lo = jnp.concatenate(a[:SUBLANES], axis=0)
        a_hi = jnp.concatenate(a[SUBLANES:2 * SUBLANES], axis=0)
        xs = [jnp.where(sub < n // l, a_lo + b[l - 1], NEG_INF) for l in range(1, n + 1)]
        zs = [a_hi + b[0], jnp.where(sub < 1, a[n - 1] + b[0], NEG_INF)]
        best = _merge_top([xs, zs], n)
        z = jnp.zeros_like(best[0])
        for r in range(PEER_TOPK):
            z = z + jnp.exp(best[r] - best[0])
        tau = 0.5 * (best[PEER_TOPK - 1] + best[PEER_TOPK])
        d1 = s1 - b[0]
        d0 = s0 - a[0]
        tau_rel = tau - best[0]
        scale = PEER_QONE / jnp.maximum(-tau_rel, 1e-30)
        s1_ref[0, h] = jnp.maximum(d1 * scale, -2.0 * PEER_QONE).astype(jnp.int32).astype(jnp.int16)
        th_ref[0, h] = jnp.minimum((tau_rel - d0) * scale, 2.0 * PEER_QONE - 1.0).astype(jnp.int32)
        e1_ref[0, h] = jnp.exp(d1).astype(BF16)
        c_ref[0, h] = jnp.exp(d0) * (0.5 / z)

    def head_pair(i, _):
        head(2 * i)
        head(2 * i + 1)
        return 0

    lax.fori_loop(0, PEER_HEADS // 2, head_pair, 0)


def _peer_sel(x1b, wq, sk, tt):
    T, D = x1b.shape
    qd = wq.shape[1]
    shp = [jax.ShapeDtypeStruct((T // tt, PEER_HEADS, PEER_NKEYS, tt), dt)
           for dt in (jnp.int16, BF16, jnp.int32, F32)]
    ospec = pl.BlockSpec((1, PEER_HEADS, PEER_NKEYS, tt), lambda i: (i, 0, 0, 0))
    return pl.pallas_call(
        _peer_sel_kernel,
        grid=(T // tt,),
        in_specs=[pl.BlockSpec((tt, D), lambda i: (i, 0)),
                  pl.BlockSpec((D, qd), lambda i: (0, 0)),
                  pl.BlockSpec((2 * PEER_HEADS, PEER_NKEYS, PEER_HALF), lambda i: (0, 0, 0))],
        out_specs=[ospec, ospec, ospec, ospec],
        out_shape=shp,
        scratch_shapes=[pltpu.VMEM((2 * PEER_HEADS, tt, PEER_HALF), BF16)],
        compiler_params=pltpu.CompilerParams(dimension_semantics=("arbitrary",),
                                             vmem_limit_bytes=VMEM_LIMIT),
        name="peer_sel",
    )(x1b, wq, sk)


PEER_ET = 2048
PEER_EQ = 512
PEER_LC = 256


def _peer_ffn_kernel(xt_ref, x1_ref, u_ref, vt_ref, s1_ref, e1_ref, th_ref, c_ref, g_ref, b_ref,
                     o_ref, acc_ref, h_ref, p_ref):
    e = pl.program_id(1)
    n_lc = xt_ref.shape[0]
    n_q = PEER_ET // PEER_EQ
    n_k = PEER_EQ // PEER_NKEYS
    assert n_lc == 2

    @pl.when(e == 0)
    def _():
        acc_ref[...] = jnp.zeros_like(acc_ref)

    def scores(q, lc):
        r0 = q * PEER_EQ if isinstance(q, int) else pl.multiple_of(q * PEER_EQ, PEER_EQ)
        h_ref[lc] = _dot(u_ref[pl.ds(r0, PEER_EQ), :], xt_ref[lc])

    def gated_act(q, lc):
        reps = PEER_NKEYS // PACKED_ROWS
        for k in range(n_k):
            rows = slice(k * PEER_NKEYS, (k + 1) * PEER_NKEYS)
            gate = jnp.zeros((PEER_NKEYS, PEER_LC), BF16)
            for h in range(PEER_HEADS):
                th = th_ref[lc, h, pl.ds(q * n_k + k, 1), :]
                cc = c_ref[lc, h, pl.ds(q * n_k + k, 1), :]
                th = jnp.tile(jnp.broadcast_to(th, (PACKED_ROWS, PEER_LC)).astype(jnp.int16), (reps, 1))
                cc = jnp.tile(jnp.broadcast_to(cc, (PACKED_ROWS, PEER_LC)).astype(BF16), (reps, 1))
                val = e1_ref[lc, h] * cc
                gate = gate + jnp.where(s1_ref[lc, h] >= th, val, jnp.zeros_like(val))
            hh = h_ref[lc, rows, :]
            act = hh + hh * lax.erf(hh * (2.0 ** -0.5))
            p_ref[lc, rows, :] = gate * act.astype(BF16)

    def accumulate(q, lc):
        acc_ref[lc] += _dot(vt_ref[q], p_ref[lc])

    scores(0, 0)
    scores(0, 1)
    gated_act(0, 0)

    def step(q, _):
        scores(q + 1, 0)
        gated_act(q, 1)
        accumulate(q, 0)
        scores(q + 1, 1)
        gated_act(q + 1, 0)
        accumulate(q, 1)
        return 0

    lax.fori_loop(0, n_q - 1, step, 0)
    gated_act(n_q - 1, 1)
    accumulate(n_q - 1, 0)
    accumulate(n_q - 1, 1)

    @pl.when(e == pl.num_programs(1) - 1)
    def _():
        for lc in range(n_lc):
            rows = slice(lc * PEER_LC, (lc + 1) * PEER_LC)
            y = ALPHA * x1_ref[rows, :] + acc_ref[lc].T
            o_ref[rows, :] = _layer_norm(y, g_ref[...], b_ref[...])


def _peer_ffn(xt, x1, u, vt, s1, e1, th, cc, g, b, tt):
    T, D = x1.shape
    E = u.shape[0]
    n_lc = tt // PEER_LC
    n_q = PEER_ET // PEER_EQ
    tok = pl.BlockSpec((n_lc, PEER_HEADS, PEER_NKEYS, PEER_LC), lambda t, e: (t, 0, 0, 0))
    sel = pl.BlockSpec((n_lc, PEER_HEADS, PEER_ET // PEER_NKEYS, PEER_LC), lambda t, e: (t, 0, e, 0))
    return pl.pallas_call(
        _peer_ffn_kernel,
        grid=(T // tt, E // PEER_ET),
        in_specs=[pl.BlockSpec((n_lc, D, PEER_LC), lambda t, e: (t, 0, 0)),
                  pl.BlockSpec((tt, D), lambda t, e: (t, 0)),
                  pl.BlockSpec((PEER_ET, D), lambda t, e: (e, 0)),
                  pl.BlockSpec((n_q, D, PEER_EQ), lambda t, e: (e, 0, 0)),
                  tok, tok, sel, sel,
                  pl.BlockSpec((1, D), lambda t, e: (0, 0)),
                  pl.BlockSpec((1, D), lambda t, e: (0, 0))],
        out_specs=pl.BlockSpec((tt, D), lambda t, e: (t, 0)),
        out_shape=jax.ShapeDtypeStruct((T, D), F32),
        scratch_shapes=[pltpu.VMEM((n_lc, D, PEER_LC), F32),
                        pltpu.VMEM((n_lc, PEER_EQ, PEER_LC), F32),
                        pltpu.VMEM((n_lc, PEER_EQ, PEER_LC), BF16)],
        compiler_params=pltpu.CompilerParams(dimension_semantics=("arbitrary", "arbitrary"),
                                             vmem_limit_bytes=VMEM_LIMIT),
        name="peer_ffn",
    )(xt, x1, u, vt, s1, e1, th, cc, g, b)


def _layer(x, w_in, w_gla_gate, b_gla_gate, gla_norm_g, sb_norm_g, w_out, ln1_g, ln1_b,
           peer_w_query, peer_sub_keys, peer_u, peer_v, ln2_g, ln2_b):
    B, S, D = x.shape
    T = B * S
    x2d = x.reshape(T, D)

    kw, gw, sw = GLA_HEADS * GLA_DK, GLA_HEADS * GLA_DV, SB_HEADS * SB_DH
    pts = [0, kw, 2 * kw, 2 * kw + gw, 2 * kw + 2 * gw, 2 * kw + 2 * gw + GLA_RANK]
    pts += [pts[-1] + sw, pts[-1] + 2 * sw, pts[-1] + 3 * sw]
    cols = [w_in[:, pts[n]:pts[n + 1]] for n in range(8)]
    cols[4] = jnp.pad(cols[4], ((0, 0), (0, LANES - GLA_RANK)))
    w_cat = jnp.concatenate(cols, axis=1).astype(BF16)
    wg = jnp.pad(w_gla_gate, ((0, LANES - GLA_RANK), (0, 0))).astype(BF16)

    qg, kg, vg, gg, alr, qs, ks, vs = _inproj(x2d, w_cat, min(TOKEN_TILE, T))

    ogla = _gla(qg, kg, vg, gg, alr, wg, b_gla_gate.reshape(1, -1), gla_norm_g.reshape(1, -1),
                B, S, min(GLA_SEQ_BLOCK, S))

    r = lax.broadcasted_iota(jnp.int32, (2 * SB_T, 2 * SB_T), 0) % SB_T
    c = lax.broadcasted_iota(jnp.int32, (2 * SB_T, 2 * SB_T), 1)
    tu = jnp.where((c >= SB_T) | (r > c), 1.0, 0.0).astype(BF16)
    osb = _sb(qs, ks, vs, tu, sb_norm_g.reshape(1, -1), B, S)

    x1, x1b = _outproj(ogla, osb, x2d, w_out.astype(BF16), ln1_g.reshape(1, D), ln1_b.reshape(1, D),
                       min(TOKEN_TILE, T))

    sk = peer_sub_keys.reshape(2 * PEER_HEADS, PEER_NKEYS, PEER_HALF).astype(BF16)
    s1, e1, th, cc = _peer_sel(x1b, peer_w_query.astype(BF16), sk, PEER_LC)

    xt = x1b.reshape(-1, PEER_LC, D).transpose(0, 2, 1)
    vt = peer_v.reshape(-1, PEER_EQ, D).transpose(0, 2, 1).astype(BF16)
    out = _peer_ffn(xt, x1, peer_u.astype(BF16), vt, s1, e1, th, cc,
                    ln2_g.reshape(1, D), ln2_b.reshape(1, D), 2 * PEER_LC)
    return out.reshape(B, S, D)


def kernel(x, w_in, w_gla_gate, b_gla_gate, gla_norm_g, sb_norm_g, w_out, ln1_g, ln1_b,
           peer_w_query, peer_sub_keys, peer_u, peer_v, ln2_g, ln2_b):
    for l in range(DEPTH):
        x = _layer(x, w_in[l], w_gla_gate[l], b_gla_gate[l], gla_norm_g[l], sb_norm_g[l], w_out[l],
                   ln1_g[l], ln1_b[l], peer_w_query[l], peer_sub_keys[l], peer_u[l], peer_v[l],
                   ln2_g[l], ln2_b[l])
    return x
```

```python
import functools

import jax
import jax.numpy as jnp
from jax import lax
from jax.experimental import pallas as pl
from jax.experimental.pallas import tpu as pltpu

F32 = jnp.float32
BF16 = jnp.bfloat16

LANES = 128
GLA_HEADS = 4
GLA_DK = 64
GLA_DV = 128
GLA_RANK = 16
GLA_GATE_TEMP = 16.0
GLA_CHUNK = 64
SB_HEADS = 8
SB_DH = 64
PEER_HEADS = 8
PEER_NKEYS = 128
PEER_HALF = 128
PEER_TOPK = 16
PEER_QONE = 16384.0
DEPTH = 1
ALPHA = (2.0 * DEPTH) ** 0.25
EPS = 1e-5
NEG_INF = float("-inf")
SB_DEAD = -104.0

VMEM_CAPACITY_V7X = 64 * 1024 * 1024
VMEM_LIMIT = VMEM_CAPACITY_V7X * 7 // 8
TOKEN_TILE = 512
GLA_SEQ_BLOCK = 512


def _dot(a, b):
    return jnp.dot(a, b, preferred_element_type=F32)


def _dot_nt(a, b):
    return lax.dot_general(a, b, (((1,), (1,)), ((), ())), preferred_element_type=F32)


def _dot_tn(a, b):
    return lax.dot_general(a, b, (((0,), (0,)), ((), ())), preferred_element_type=F32)


def _split_bf16(v):
    hi = v.astype(BF16)
    lo = (v - hi.astype(F32)).astype(BF16)
    return hi, lo


def _log_sigmoid(z):
    return jnp.minimum(z, 0.0) - jnp.log(1.0 + jnp.exp(-jnp.abs(z)))


IN_SPLITS = (("qg", 256, F32), ("kg", 256, F32), ("vg", 512, BF16), ("gg", 512, F32),
             ("alr", LANES, F32), ("qs", 512, BF16), ("ks", 512, BF16), ("vs", 512, BF16))


def _inproj_kernel(x_ref, w_ref, *out_refs):
    xb = x_ref[...].astype(BF16)
    off = 0
    for (name, width, _), o_ref in zip(IN_SPLITS, out_refs):
        r = _dot(xb, w_ref[:, off:off + width])
        if name in ("qg", "qs"):
            r = r * (GLA_DK ** -0.5)
        o_ref[...] = r.astype(o_ref.dtype)
        off += width


def _inproj(x2d, w_cat, tm):
    T, D = x2d.shape
    wtot = w_cat.shape[1]
    return pl.pallas_call(
        _inproj_kernel,
        grid=(T // tm,),
        in_specs=[pl.BlockSpec((tm, D), lambda i: (i, 0)),
                  pl.BlockSpec((D, wtot), lambda i: (0, 0))],
        out_specs=[pl.BlockSpec((tm, w), lambda i: (i, 0)) for _, w, _ in IN_SPLITS],
        out_shape=[jax.ShapeDtypeStruct((T, w), dt) for _, w, dt in IN_SPLITS],
        compiler_params=pltpu.CompilerParams(dimension_semantics=("arbitrary",),
                                             vmem_limit_bytes=VMEM_LIMIT),
        name="inproj",
    )(x2d, w_cat)


def _gla_kernel(q_ref, k_ref, v_ref, g_ref, a_ref, wg_ref, bg_ref, gn_ref, o_ref, st_ref, *, n_chunks):
    @pl.when(pl.program_id(1) == 0)
    def _():
        st_ref[...] = jnp.zeros_like(st_ref)

    C = GLA_CHUNK
    KW, VW = 2 * GLA_DK, 2 * GLA_DV
    lane = lax.broadcasted_iota(jnp.int32, (1, KW), 1)
    head_mask = [lane < GLA_DK, lane >= GLA_DK]
    r_i = lax.broadcasted_iota(jnp.int32, (C, C), 0)
    c_i = lax.broadcasted_iota(jnp.int32, (C, C), 1)
    causal = c_i <= r_i
    tri = jnp.where(causal, 1.0, 0.0).astype(BF16)

    la_all = _log_sigmoid(_dot(a_ref[...].astype(BF16), wg_ref[...]) + bg_ref[...]) * (1.0 / GLA_GATE_TEMP)

    for c in range(n_chunks):
        rows = slice(c * C, (c + 1) * C)
        la = la_all[rows]
        hi, lo = _split_bf16(la)
        b_all = _dot(tri, hi) + _dot(tri, lo)
        for hp in range(GLA_HEADS // 2):
            kcols = slice(hp * KW, (hp + 1) * KW)
            vcols = slice(hp * VW, (hp + 1) * VW)
            b = b_all[:, kcols]
            b_last = b[C - 1:C, :]
            q = q_ref[rows, kcols]
            k = k_ref[rows, kcols]
            q_in = q * jnp.exp(b)
            k_in = (k * jnp.exp(-b)).astype(BF16)
            k_st = (k * jnp.exp(b_last - b)).astype(BF16)
            dec = jnp.exp(b_last)
            v = v_ref[rows, vcols]
            st = st_ref[hp]
            st_b = st.astype(BF16)
            outs = []
            for h in range(2):
                qm = jnp.where(head_mask[h], q_in, 0.0).astype(BF16)
                att = jnp.where(causal, _dot_nt(qm, k_in), 0.0)
                vh = v[:, h * GLA_DV:(h + 1) * GLA_DV]
                o_h = _dot(att.astype(BF16), vh) + _dot_nt(qm, st_b[h * GLA_DV:(h + 1) * GLA_DV, :])
                ms = jnp.mean(o_h * o_h, axis=-1, keepdims=True)
                outs.append(o_h * lax.rsqrt(ms + EPS))
            st_ref[hp] = st * dec + _dot_tn(v, k_st)
            o = jnp.concatenate(outs, axis=1) * gn_ref[:, vcols]
            g = g_ref[rows, vcols]
            o_ref[rows, vcols] = (o * (g * jax.nn.sigmoid(g))).astype(o_ref.dtype)


def _gla(qg, kg, vg, gg, alr, wg, bg, gn, B, S, ls):
    n_s = S // ls
    kw, vw = GLA_HEADS * GLA_DK, GLA_HEADS * GLA_DV
    row = lambda b, s: (b * n_s + s, 0)
    par = lambda b, s: (0, 0)
    return pl.pallas_call(
        functools.partial(_gla_kernel, n_chunks=ls // GLA_CHUNK),
        grid=(B, n_s),
        in_specs=[pl.BlockSpec((ls, kw), row),
                  pl.BlockSpec((ls, kw), row),
                  pl.BlockSpec((ls, vw), row),
                  pl.BlockSpec((ls, vw), row),
                  pl.BlockSpec((ls, LANES), row),
                  pl.BlockSpec((LANES, kw), par),
                  pl.BlockSpec((1, kw), par),
                  pl.BlockSpec((1, vw), par)],
        out_specs=pl.BlockSpec((ls, vw), row),
        out_shape=jax.ShapeDtypeStruct((B * S, vw), BF16),
        scratch_shapes=[pltpu.VMEM((GLA_HEADS // 2, 2 * GLA_DV, 2 * GLA_DK), F32)],
        compiler_params=pltpu.CompilerParams(dimension_semantics=("arbitrary", "arbitrary"),
                                             vmem_limit_bytes=VMEM_LIMIT),
        name="gla",
    )(qg, kg, vg, gg, alr, wg, bg, gn)


SB_T = 128
SB_QB = 2
SB_GW = 2 * LANES
SB_GROUPS = SB_HEADS * SB_DH // SB_GW
SB_TILES = 4 * SB_GROUPS * SB_QB


def _sb_kernel(q_ref, k_ref, v_ref, tu_ref, gn_ref, o_ref,
               acc_ref, car_ref, kb_ref, vb_ref, zl_ref, l_ref, cs_ref):
    i = pl.program_id(1)
    T = SB_T
    lane = lax.broadcasted_iota(jnp.int32, (1, SB_GW), 1)
    is_h0 = (lane % LANES) < SB_DH
    qg = []
    for a in range(SB_QB):
        qa = []
        for g in range(SB_GROUPS):
            q = q_ref[a * T:(a + 1) * T, g * SB_GW:(g + 1) * SB_GW]
            zero = jnp.zeros_like(q)
            qa.append(jnp.concatenate([jnp.where(is_h0, q, zero), jnp.where(is_h0, zero, q)], axis=0))
        qg.append(qa)
    row = lax.broadcasted_iota(jnp.int32, (T, T), 0)
    col = lax.broadcasted_iota(jnp.int32, (T, T), 1)
    acc_ref[...] = jnp.zeros_like(acc_ref)
    car_ref[...] = jnp.zeros_like(car_ref)
    kb_ref[...] = jnp.zeros_like(kb_ref)
    vb_ref[...] = jnp.zeros_like(vb_ref)

    def sweep(m, diagonal):
        valid = col < row
        for a in range(SB_QB):
            j = SB_QB * i + a - m
            ks = pl.multiple_of(jnp.maximum(j, 0) * T, T)
            for g in range(SB_GROUPS):
                s = a * SB_GROUPS + g
                for p in range(2):
                    src = slice(g * SB_GW + p * LANES, g * SB_GW + (p + 1) * LANES)
                    kb_ref[s, p * T:(p + 1) * T, p * LANES:(p + 1) * LANES] = k_ref[pl.ds(ks, T), src]
                    vb_ref[s, p * T:(p + 1) * T, p * LANES:(p + 1) * LANES] = v_ref[pl.ds(ks, T), src]
                z = _dot_nt(qg[a][g], kb_ref[s])
                for h in range(2):
                    for p in range(2):
                        n = 4 * s + 2 * h + p
                        zq = z[h * T:(h + 1) * T, p * T:(p + 1) * T]
                        lnb = -(jnp.maximum(zq, 0.0) + jnp.log(1.0 + jnp.exp(-jnp.abs(zq))))
                        if diagonal:
                            lnb = jnp.where(valid, lnb, 0.0)
                        hi, lo = _split_bf16(lnb)
                        l_ref[n * T:(n + 1) * T, :T] = hi
                        l_ref[n * T:(n + 1) * T, T:] = lo
                        zl_ref[n] = zq + lnb
        cs_ref[...] = _dot(l_ref[...], tu_ref[...])
        top = None
        for a in range(SB_QB):
            j = SB_QB * i + a - m
            spent = jnp.where(j < 0, -1e30, 0.0).astype(F32)
            for g in range(SB_GROUPS):
                s = a * SB_GROUPS + g
                w_rows = []
                for h in range(2):
                    w_cols = []
                    for p in range(2):
                        n = 4 * s + 2 * h + p
                        car = car_ref[n] if diagonal else car_ref[n] + spent
                        w = jnp.exp(zl_ref[n] + cs_ref[n * T:(n + 1) * T, :T] + car)
                        if diagonal:
                            w = jnp.where(valid, w, 0.0)
                        w_cols.append(w.astype(BF16))
                        car = car + cs_ref[n * T:(n + 1) * T, T:]
                        car_ref[n] = car
                        top = car if top is None else jnp.maximum(top, car)
                    w_rows.append(jnp.concatenate(w_cols, axis=1))
                acc_ref[s] += _dot(jnp.concatenate(w_rows, axis=0), vb_ref[s])
        return jnp.max(top)

    def cond(carry):
        m, alive = carry
        return jnp.logical_and(m <= SB_QB * i + SB_QB - 1, alive > SB_DEAD)

    def body(carry):
        m, _ = carry
        return m + 1, sweep(m, diagonal=False)

    lax.while_loop(cond, body, (jnp.int32(1), sweep(0, diagonal=True)))

    head_of_lane = lane // SB_DH
    for a in range(SB_QB):
        for g in range(SB_GROUPS):
            acc = acc_ref[a * SB_GROUPS + g]
            o = jnp.where(is_h0, acc[:T], acc[T:])
            sq = o * o
            ms = jnp.zeros_like(o)
            for hd in range(SB_GW // SB_DH):
                mine = head_of_lane == hd
                ms = jnp.where(mine, jnp.sum(jnp.where(mine, sq, 0.0), axis=-1, keepdims=True), ms)
            cols = slice(g * SB_GW, (g + 1) * SB_GW)
            o_ref[a * T:(a + 1) * T, cols] = (o * lax.rsqrt(ms * (1.0 / SB_DH) + EPS) * gn_ref[:, cols]).astype(o_ref.dtype)


def _sb(qs, ks, vs, tu, gn, B, S):
    rows = SB_QB * SB_T
    n_q = S // rows
    width = SB_HEADS * SB_DH
    n_slab = SB_QB * SB_GROUPS
    return pl.pallas_call(
        _sb_kernel,
        grid=(B, n_q),
        in_specs=[pl.BlockSpec((rows, width), lambda b, i: (b * n_q + i, 0)),
                  pl.BlockSpec((S, width), lambda b, i: (b, 0)),
                  pl.BlockSpec((S, width), lambda b, i: (b, 0)),
                  pl.BlockSpec((2 * SB_T, 2 * SB_T), lambda b, i: (0, 0)),
                  pl.BlockSpec((1, width), lambda b, i: (0, 0))],
        out_specs=pl.BlockSpec((rows, width), lambda b, i: (b * n_q + i, 0)),
        out_shape=jax.ShapeDtypeStruct((B * S, width), BF16),
        scratch_shapes=[pltpu.VMEM((n_slab, 2 * SB_T, SB_GW), F32),
                        pltpu.VMEM((SB_TILES, SB_T, SB_T), F32),
                        pltpu.VMEM((n_slab, 2 * SB_T, SB_GW), BF16),
                        pltpu.VMEM((n_slab, 2 * SB_T, SB_GW), BF16),
                        pltpu.VMEM((SB_TILES, SB_T, SB_T), F32),
                        pltpu.VMEM((SB_TILES * SB_T, 2 * SB_T), BF16),
                        pltpu.VMEM((SB_TILES * SB_T, 2 * SB_T), F32)],
        compiler_params=pltpu.CompilerParams(dimension_semantics=("arbitrary", "arbitrary"),
                                             vmem_limit_bytes=VMEM_LIMIT),
        name="sb",
    )(qs, ks, vs, tu, gn)


def _layer_norm(y, g, b):
    mu = jnp.mean(y, axis=-1, keepdims=True)
    d = y - mu
    var = jnp.mean(d * d, axis=-1, keepdims=True)
    return d * lax.rsqrt(var + EPS) * g + b


def _outproj_kernel(og_ref, os_ref, x_ref, wo_ref, g_ref, b_ref, x1_ref, x1b_ref):
    half = og_ref.shape[1]
    mix = _dot(og_ref[...], wo_ref[:half, :]) + _dot(os_ref[...], wo_ref[half:, :])
    x1 = _layer_norm(ALPHA * x_ref[...] + mix, g_ref[...], b_ref[...])
    x1_ref[...] = x1
    x1b_ref[...] = x1.astype(BF16)


def _outproj(ogla, osb, x2d, wo, g, b, tm):
    T, D = x2d.shape
    half = ogla.shape[1]
    return pl.pallas_call(
        _outproj_kernel,
        grid=(T // tm,),
        in_specs=[pl.BlockSpec((tm, half), lambda i: (i, 0)),
                  pl.BlockSpec((tm, half), lambda i: (i, 0)),
                  pl.BlockSpec((tm, D), lambda i: (i, 0)),
                  pl.BlockSpec((2 * half, D), lambda i: (0, 0)),
                  pl.BlockSpec((1, D), lambda i: (0, 0)),
                  pl.BlockSpec((1, D), lambda i: (0, 0))],
        out_specs=[pl.BlockSpec((tm, D), lambda i: (i, 0)),
                   pl.BlockSpec((tm, D), lambda i: (i, 0))],
        out_shape=[jax.ShapeDtypeStruct((T, D), F32), jax.ShapeDtypeStruct((T, D), BF16)],
        compiler_params=pltpu.CompilerParams(dimension_semantics=("arbitrary",),
                                             vmem_limit_bytes=VMEM_LIMIT),
        name="outproj",
    )(ogla, osb, x2d, wo, g, b)


SUBLANES = 8
PACKED_ROWS = 16


def _batcher_network(n):
    pairs, p = [], 1
    while p < n:
        k = p
        while k >= 1:
            for j in range(k % p, n - k, 2 * k):
                for i in range(min(k, n - j - k)):
                    if (i + j) // (2 * p) == (i + j + k) // (2 * p):
                        pairs.append((i + j, i + j + k))
            k //= 2
        p *= 2
    return pairs


def _merge_top(stacks, n):
    stacks = [list(st) for st in stacks]
    tops = []
    for r in range(n):
        head = stacks[0][0]
        for st in stacks[1:]:
            head = jnp.maximum(head, st[0])
        m = jnp.max(head, axis=0, keepdims=True)
        tops.append(m)
        need = n - r - 1
        for st in stacks:
            if need == 0:
                break
            hit = st[0] == m
            depth = min(need, len(st))
            for d in range(depth):
                below = st[d + 1] if d + 1 < len(st) else NEG_INF
                st[d] = jnp.where(hit, below, st[d])
            del st[depth:]
    return tops


def _top_rows(s, n):
    slabs = [s[r:r + SUBLANES] for r in range(0, s.shape[0], SUBLANES)]
    for i, j in _batcher_network(len(slabs)):
        slabs[i], slabs[j] = jnp.maximum(slabs[i], slabs[j]), jnp.minimum(slabs[i], slabs[j])
    return _merge_top([slabs], n)


def _peer_sel_kernel(x_ref, wq_ref, sk_ref, s1_ref, e1_ref, th_ref, c_ref, q_scr):
    tt = x_ref.shape[0]
    qry = _dot(x_ref[...], wq_ref[...]).astype(BF16)
    for hp in range(2 * PEER_HEADS):
        q_scr[hp] = qry[:, hp * PEER_HALF:(hp + 1) * PEER_HALF]
    sub = lax.broadcasted_iota(jnp.int32, (SUBLANES, tt), 0)

    def head(h):
        s0 = _dot_nt(sk_ref[2 * h], q_scr[2 * h])
        s1 = _dot_nt(sk_ref[2 * h + 1], q_scr[2 * h + 1])
        n = PEER_TOPK + 1
        a = _top_rows(s0, n)
        b = _top_rows(s1, n)
        a_lo = jnp.concatenate(a[:SUBLANES], axis=0)
        a_hi = jnp.concatenate(a[SUBLANES:2 * SUBLANES], axis=0)
        xs = [jnp.where(sub < n // l, a_lo + b[l - 1], NEG_INF) for l in range(1, n + 1)]
        zs = [a_hi + b[0], jnp.where(sub < 1, a[n - 1] + b[0], NEG_INF)]
        best = _merge_top([xs, zs], n)
        z = jnp.zeros_like(best[0])
        for r in range(PEER_TOPK):
            z = z + jnp.exp(best[r] - best[0])
        tau = 0.5 * (best[PEER_TOPK - 1] + best[PEER_TOPK])
        d1 = s1 - b[0]
        d0 = s0 - a[0]
        tau_rel = tau - best[0]
        scale = PEER_QONE / jnp.maximum(-tau_rel, 1e-30)
        s1_ref[0, h] = jnp.maximum(d1 * scale, -2.0 * PEER_QONE).astype(jnp.int32).astype(jnp.int16)
        th_ref[0, h] = jnp.minimum((tau_rel - d0) * scale, 2.0 * PEER_QONE - 1.0).astype(jnp.int32)
        e1_ref[0, h] = jnp.exp(d1).astype(BF16)
        c_ref[0, h] = jnp.exp(d0) * (0.5 / z)

    def head_pair(i, _):
        head(2 * i)
        head(2 * i + 1)
        return 0

    lax.fori_loop(0, PEER_HEADS // 2, head_pair, 0)


def _peer_sel(x1b, wq, sk, tt):
    T, D = x1b.shape
    qd = wq.shape[1]
    shp = [jax.ShapeDtypeStruct((T // tt, PEER_HEADS, PEER_NKEYS, tt), dt)
           for dt in (jnp.int16, BF16, jnp.int32, F32)]
    ospec = pl.BlockSpec((1, PEER_HEADS, PEER_NKEYS, tt), lambda i: (i, 0, 0, 0))
    return pl.pallas_call(
        _peer_sel_kernel,
        grid=(T // tt,),
        in_specs=[pl.BlockSpec((tt, D), lambda i: (i, 0)),
                  pl.BlockSpec((D, qd), lambda i: (0, 0)),
                  pl.BlockSpec((2 * PEER_HEADS, PEER_NKEYS, PEER_HALF), lambda i: (0, 0, 0))],
        out_specs=[ospec, ospec, ospec, ospec],
        out_shape=shp,
        scratch_shapes=[pltpu.VMEM((2 * PEER_HEADS, tt, PEER_HALF), BF16)],
        compiler_params=pltpu.CompilerParams(dimension_semantics=("arbitrary",),
                                             vmem_limit_bytes=VMEM_LIMIT),
        name="peer_sel",
    )(x1b, wq, sk)


PEER_ET = 2048
PEER_EQ = 512
PEER_LC = 256


def _peer_ffn_kernel(xt_ref, x1_ref, u_ref, vt_ref, s1_ref, e1_ref, th_ref, c_ref, g_ref, b_ref,
                     o_ref, acc_ref, h_ref, p_ref):
    e = pl.program_id(1)
    n_lc = xt_ref.shape[0]
    n_q = PEER_ET // PEER_EQ
    n_k = PEER_EQ // PEER_NKEYS
    assert n_lc == 2

    @pl.when(e == 0)
    def _():
        acc_ref[...] = jnp.zeros_like(acc_ref)

    def scores(q, lc):
        r0 = q * PEER_EQ if isinstance(q, int) else pl.multiple_of(q * PEER_EQ, PEER_EQ)
        h_ref[lc] = _dot(u_ref[pl.ds(r0, PEER_EQ), :], xt_ref[lc])

    def gated_act(q, lc):
        reps = PEER_NKEYS // PACKED_ROWS
        for k in range(n_k):
            rows = slice(k * PEER_NKEYS, (k + 1) * PEER_NKEYS)
            gate = jnp.zeros((PEER_NKEYS, PEER_LC), BF16)
            for h in range(PEER_HEADS):
                th = th_ref[lc, h, pl.ds(q * n_k + k, 1), :]
                cc = c_ref[lc, h, pl.ds(q * n_k + k, 1), :]
                th = jnp.tile(jnp.broadcast_to(th, (PACKED_ROWS, PEER_LC)).astype(jnp.int16), (reps, 1))
                cc = jnp.tile(jnp.broadcast_to(cc, (PACKED_ROWS, PEER_LC)).astype(BF16), (reps, 1))
                val = e1_ref[lc, h] * cc
                gate = gate + jnp.where(s1_ref[lc, h] >= th, val, jnp.zeros_like(val))
            hh = h_ref[lc, rows, :]
            act = hh + hh * lax.erf(hh * (2.0 ** -0.5))
            p_ref[lc, rows, :] = gate * act.astype(BF16)

    def accumulate(q, lc):
        acc_ref[lc] += _dot(vt_ref[q], p_ref[lc])

    scores(0, 0)
    scores(0, 1)
    gated_act(0, 0)

    def step(q, _):
        scores(q + 1, 0)
        gated_act(q, 1)
        accumulate(q, 0)
        scores(q + 1, 1)
        gated_act(q + 1, 0)
        accumulate(q, 1)
        return 0

    lax.fori_loop(0, n_q - 1, step, 0)
    gated_act(n_q - 1, 1)
    accumulate(n_q - 1, 0)
    accumulate(n_q - 1, 1)

    @pl.when(e == pl.num_programs(1) - 1)
    def _():
        for lc in range(n_lc):
            rows = slice(lc * PEER_LC, (lc + 1) * PEER_LC)
            y = ALPHA * x1_ref[rows, :] + acc_ref[lc].T
            o_ref[rows, :] = _layer_norm(y, g_ref[...], b_ref[...])


def _peer_ffn(xt, x1, u, vt, s1, e1, th, cc, g, b, tt):
    T, D = x1.shape
    E = u.shape[0]
    n_lc = tt // PEER_LC
    n_q = PEER_ET // PEER_EQ
    tok = pl.BlockSpec((n_lc, PEER_HEADS, PEER_NKEYS, PEER_LC), lambda t, e: (t, 0, 0, 0))
    sel = pl.BlockSpec((n_lc, PEER_HEADS, PEER_ET // PEER_NKEYS, PEER_LC), lambda t, e: (t, 0, e, 0))
    return pl.pallas_call(
        _peer_ffn_kernel,
        grid=(T // tt, E // PEER_ET),
        in_specs=[pl.BlockSpec((n_lc, D, PEER_LC), lambda t, e: (t, 0, 0)),
                  pl.BlockSpec((tt, D), lambda t, e: (t, 0)),
                  pl.BlockSpec((PEER_ET, D), lambda t, e: (e, 0)),
                  pl.BlockSpec((n_q, D, PEER_EQ), lambda t, e: (e, 0, 0)),
                  tok, tok, sel, sel,
                  pl.BlockSpec((1, D), lambda t, e: (0, 0)),
                  pl.BlockSpec((1, D), lambda t, e: (0, 0))],
        out_specs=pl.BlockSpec((tt, D), lambda t, e: (t, 0)),
        out_shape=jax.ShapeDtypeStruct((T, D), F32),
        scratch_shapes=[pltpu.VMEM((n_lc, D, PEER_LC), F32),
                        pltpu.VMEM((n_lc, PEER_EQ, PEER_LC), F32),
                        pltpu.VMEM((n_lc, PEER_EQ, PEER_LC), BF16)],
        compiler_params=pltpu.CompilerParams(dimension_semantics=("arbitrary", "arbitrary"),
                                             vmem_limit_bytes=VMEM_LIMIT),
        name="peer_ffn",
    )(xt, x1, u, vt, s1, e1, th, cc, g, b)


def _layer(x, w_in, w_gla_gate, b_gla_gate, gla_norm_g, sb_norm_g, w_out, ln1_g, ln1_b,
           peer_w_query, peer_sub_keys, peer_u, peer_v, ln2_g, ln2_b):
    B, S, D = x.shape
    T = B * S
    x2d = x.reshape(T, D)

    kw, gw, sw = GLA_HEADS * GLA_DK, GLA_HEADS * GLA_DV, SB_HEADS * SB_DH
    pts = [0, kw, 2 * kw, 2 * kw + gw, 2 * kw + 2 * gw, 2 * kw + 2 * gw + GLA_RANK]
    pts += [pts[-1] + sw, pts[-1] + 2 * sw, pts[-1] + 3 * sw]
    cols = [w_in[:, pts[n]:pts[n + 1]] for n in range(8)]
    cols[4] = jnp.pad(cols[4], ((0, 0), (0, LANES - GLA_RANK)))
    w_cat = jnp.concatenate(cols, axis=1).astype(BF16)
    wg = jnp.pad(w_gla_gate, ((0, LANES - GLA_RANK), (0, 0))).astype(BF16)

    qg, kg, vg, gg, alr, qs, ks, vs = _inproj(x2d, w_cat, min(TOKEN_TILE, T))

    ogla = _gla(qg, kg, vg, gg, alr, wg, b_gla_gate.reshape(1, -1), gla_norm_g.reshape(1, -1),
                B, S, min(GLA_SEQ_BLOCK, S))

    r = lax.broadcasted_iota(jnp.int32, (2 * SB_T, 2 * SB_T), 0) % SB_T
    c = lax.broadcasted_iota(jnp.int32, (2 * SB_T, 2 * SB_T), 1)
    tu = jnp.where((c >= SB_T) | (r > c), 1.0, 0.0).astype(BF16)
    osb = _sb(qs, ks, vs, tu, sb_norm_g.reshape(1, -1), B, S)

    x1, x1b = _outproj(ogla, osb, x2d, w_out.astype(BF16), ln1_g.reshape(1, D), ln1_b.reshape(1, D),
                       min(TOKEN_TILE, T))

    sk = peer_sub_keys.reshape(2 * PEER_HEADS, PEER_NKEYS, PEER_HALF).astype(BF16)
    s1, e1, th, cc = _peer_sel(x1b, peer_w_query.astype(BF16), sk, PEER_LC)

    xt = x1b.reshape(-1, PEER_LC, D).transpose(0, 2, 1)
    vt = peer_v.reshape(-1, PEER_EQ, D).transpose(0, 2, 1).astype(BF16)
    out = _peer_ffn(xt, x1, peer_u.astype(BF16), vt, s1, e1, th, cc,
                    ln2_g.reshape(1, D), ln2_b.reshape(1, D), 2 * PEER_LC)
    return out.reshape(B, S, D)


def kernel(x, w_in, w_gla_gate, b_gla_gate, gla_norm_g, sb_norm_g, w_out, ln1_g, ln1_b,
           peer_w_query, peer_sub_keys, peer_u, peer_v, ln2_g, ln2_b):
    for l in range(DEPTH):
        x = _layer(x, w_in[l], w_gla_gate[l], b_gla_gate[l], gla_norm_g[l], sb_norm_g[l], w_out[l],
                   ln1_g[l], ln1_b[l], peer_w_query[l], peer_sub_keys[l], peer_u[l], peer_v[l],
                   ln2_g[l], ln2_b[l])
    return x
```

```python
import functools

import jax
import jax.numpy as jnp
from jax import lax
from jax.experimental import pallas as pl
from jax.experimental.pallas import tpu as pltpu

F32 = jnp.float32
BF16 = jnp.bfloat16

LANES = 128
GLA_HEADS = 4
GLA_DK = 64
GLA_DV = 128
GLA_RANK = 16
GLA_GATE_TEMP = 16.0
GLA_CHUNK = 64
SB_HEADS = 8
SB_DH = 64
PEER_HEADS = 8
PEER_NKEYS = 128
PEER_HALF = 128
PEER_TOPK = 16
PEER_QONE = 16384.0
DEPTH = 1
ALPHA = (2.0 * DEPTH) ** 0.25
EPS = 1e-5
NEG_INF = float("-inf")
SB_DEAD = -104.0

VMEM_CAPACITY_V7X = 64 * 1024 * 1024
VMEM_LIMIT = VMEM_CAPACITY_V7X * 7 // 8
TOKEN_TILE = 512
GLA_SEQ_BLOCK = 512


def _dot(a, b):
    return jnp.dot(a, b, preferred_element_type=F32)


def _dot_nt(a, b):
    return lax.dot_general(a, b, (((1,), (1,)), ((), ())), preferred_element_type=F32)


def _dot_tn(a, b):
    return lax.dot_general(a, b, (((0,), (0,)), ((), ())), preferred_element_type=F32)


def _split_bf16(v):
    hi = v.astype(BF16)
    lo = (v - hi.astype(F32)).astype(BF16)
    return hi, lo


def _log_sigmoid(z):
    return jnp.minimum(z, 0.0) - jnp.log(1.0 + jnp.exp(-jnp.abs(z)))


IN_SPLITS = (("qg", 256, F32), ("kg", 256, F32), ("vg", 512, BF16), ("gg", 512, F32),
             ("alr", LANES, F32), ("qs", 512, BF16), ("ks", 512, BF16), ("vs", 512, BF16))


def _inproj_kernel(x_ref, w_ref, *out_refs):
    xb = x_ref[...].astype(BF16)
    off = 0
    for (name, width, _), o_ref in zip(IN_SPLITS, out_refs):
        r = _dot(xb, w_ref[:, off:off + width])
        if name in ("qg", "qs"):
            r = r * (GLA_DK ** -0.5)
        o_ref[...] = r.astype(o_ref.dtype)
        off += width


def _inproj(x2d, w_cat, tm):
    T, D = x2d.shape
    wtot = w_cat.shape[1]
    return pl.pallas_call(
        _inproj_kernel,
        grid=(T // tm,),
        in_specs=[pl.BlockSpec((tm, D), lambda i: (i, 0)),
                  pl.BlockSpec((D, wtot), lambda i: (0, 0))],
        out_specs=[pl.BlockSpec((tm, w), lambda i: (i, 0)) for _, w, _ in IN_SPLITS],
        out_shape=[jax.ShapeDtypeStruct((T, w), dt) for _, w, dt in IN_SPLITS],
        compiler_params=pltpu.CompilerParams(dimension_semantics=("arbitrary",),
                                             vmem_limit_bytes=VMEM_LIMIT),
        name="inproj",
    )(x2d, w_cat)


def _gla_kernel(q_ref, k_ref, v_ref, g_ref, a_ref, wg_ref, bg_ref, gn_ref, o_ref, st_ref, *, n_chunks):
    @pl.when(pl.program_id(1) == 0)
    def _():
        st_ref[...] = jnp.zeros_like(st_ref)

    C = GLA_CHUNK
    KW, VW = 2 * GLA_DK, 2 * GLA_DV
    lane = lax.broadcasted_iota(jnp.int32, (1, KW), 1)
    head_mask = [lane < GLA_DK, lane >= GLA_DK]
    r_i = lax.broadcasted_iota(jnp.int32, (C, C), 0)
    c_i = lax.broadcasted_iota(jnp.int32, (C, C), 1)
    causal = c_i <= r_i
    tri = jnp.where(causal, 1.0, 0.0).astype(BF16)

    la_all = _log_sigmoid(_dot(a_ref[...].astype(BF16), wg_ref[...]) + bg_ref[...]) * (1.0 / GLA_GATE_TEMP)

    for c in range(n_chunks):
        rows = slice(c * C, (c + 1) * C)
        la = la_all[rows]
        hi, lo = _split_bf16(la)
        b_all = _dot(tri, hi) + _dot(tri, lo)
        for hp in range(GLA_HEADS // 2):
            kcols = slice(hp * KW, (hp + 1) * KW)
            vcols = slice(hp * VW, (hp + 1) * VW)
            b = b_all[:, kcols]
            b_last = b[C - 1:C, :]
            q = q_ref[rows, kcols]
            k = k_ref[rows, kcols]
            q_in = q * jnp.exp(b)
            k_in = (k * jnp.exp(-b)).astype(BF16)
            k_st = (k * jnp.exp(b_last - b)).astype(BF16)
            dec = jnp.exp(b_last)
            v = v_ref[rows, vcols]
            st = st_ref[hp]
            st_b = st.astype(BF16)
            outs = []
            for h in range(2):
                qm = jnp.where(head_mask[h], q_in, 0.0).astype(BF16)
                att = jnp.where(causal, _dot_nt(qm, k_in), 0.0)
                vh = v[:, h * GLA_DV:(h + 1) * GLA_DV]
                o_h = _dot(att.astype(BF16), vh) + _dot_nt(qm, st_b[h * GLA_DV:(h + 1) * GLA_DV, :])
                ms = jnp.mean(o_h * o_h, axis=-1, keepdims=True)
                outs.append(o_h * lax.rsqrt(ms + EPS))
            st_ref[hp] = st * dec + _dot_tn(v, k_st)
            o = jnp.concatenate(outs, axis=1) * gn_ref[:, vcols]
            g = g_ref[rows, vcols]
            o_ref[rows, vcols] = (o * (g * jax.nn.sigmoid(g))).astype(o_ref.dtype)


def _gla(qg, kg, vg, gg, alr, wg, bg, gn, B, S, ls):
    n_s = S // ls
    kw, vw = GLA_HEADS * GLA_DK, GLA_HEADS * GLA_DV
    row = lambda b, s: (b * n_s + s, 0)
    par = lambda b, s: (0, 0)
    return pl.pallas_call(
        functools.partial(_gla_kernel, n_chunks=ls // GLA_CHUNK),
        grid=(B, n_s),
        in_specs=[pl.BlockSpec((ls, kw), row),
                  pl.BlockSpec((ls, kw), row),
                  pl.BlockSpec((ls, vw), row),
                  pl.BlockSpec((ls, vw), row),
                  pl.BlockSpec((ls, LANES), row),
                  pl.BlockSpec((LANES, kw), par),
                  pl.BlockSpec((1, kw), par),
                  pl.BlockSpec((1, vw), par)],
        out_specs=pl.BlockSpec((ls, vw), row),
        out_shape=jax.ShapeDtypeStruct((B * S, vw), BF16),
        scratch_shapes=[pltpu.VMEM((GLA_HEADS // 2, 2 * GLA_DV, 2 * GLA_DK), F32)],
        compiler_params=pltpu.CompilerParams(dimension_semantics=("arbitrary", "arbitrary"),
                                             vmem_limit_bytes=VMEM_LIMIT),
        name="gla",
    )(qg, kg, vg, gg, alr, wg, bg, gn)


SB_T = 128
SB_QB = 4
SB_GW = 2 * LANES
SB_GROUPS = SB_HEADS * SB_DH // SB_GW
SB_TILES = 4 * SB_GROUPS * SB_QB


def _sb_kernel(q_ref, k_ref, v_ref, tu_ref, gn_ref, o_ref,
               acc_ref, car_ref, kb_ref, vb_ref, zl_ref, l_ref, cs_ref):
    i = pl.program_id(1)
    T = SB_T
    lane = lax.broadcasted_iota(jnp.int32, (1, SB_GW), 1)
    is_h0 = (lane % LANES) < SB_DH
    qg = []
    for a in range(SB_QB):
        qa = []
        for g in range(SB_GROUPS):
            q = q_ref[a * T:(a + 1) * T, g * SB_GW:(g + 1) * SB_GW]
            zero = jnp.zeros_like(q)
            qa.append(jnp.concatenate([jnp.where(is_h0, q, zero), jnp.where(is_h0, zero, q)], axis=0))
        qg.append(qa)
    row = lax.broadcasted_iota(jnp.int32, (T, T), 0)
    col = lax.broadcasted_iota(jnp.int32, (T, T), 1)
    acc_ref[...] = jnp.zeros_like(acc_ref)
    car_ref[...] = jnp.zeros_like(car_ref)
    kb_ref[...] = jnp.zeros_like(kb_ref)
    vb_ref[...] = jnp.zeros_like(vb_ref)

    def sweep(m, diagonal):
        valid = col < row
        for a in range(SB_QB):
            j = SB_QB * i + a - m
            ks = pl.multiple_of(jnp.maximum(j, 0) * T, T)
            for g in range(SB_GROUPS):
                s = a * SB_GROUPS + g
                for p in range(2):
                    src = slice(g * SB_GW + p * LANES, g * SB_GW + (p + 1) * LANES)
                    kb_ref[s, p * T:(p + 1) * T, p * LANES:(p + 1) * LANES] = k_ref[pl.ds(ks, T), src]
                    vb_ref[s, p * T:(p + 1) * T, p * LANES:(p + 1) * LANES] = v_ref[pl.ds(ks, T), src]
                z = _dot_nt(qg[a][g], kb_ref[s])
                for h in range(2):
                    for p in range(2):
                        n = 4 * s + 2 * h + p
                        zq = z[h * T:(h + 1) * T, p * T:(p + 1) * T]
                        lnb = -(jnp.maximum(zq, 0.0) + jnp.log(1.0 + jnp.exp(-jnp.abs(zq))))
                        if diagonal:
                            lnb = jnp.where(valid, lnb, 0.0)
                        hi, lo = _split_bf16(lnb)
                        l_ref[n * T:(n + 1) * T, :T] = hi
                        l_ref[n * T:(n + 1) * T, T:] = lo
                        zl_ref[n] = zq + lnb
        cs_ref[...] = _dot(l_ref[...], tu_ref[...])
        top = None
        for a in range(SB_QB):
            j = SB_QB * i + a - m
            spent = jnp.where(j < 0, -1e30, 0.0).astype(F32)
            for g in range(SB_GROUPS):
                s = a * SB_GROUPS + g
                w_rows = []
                for h in range(2):
                    w_cols = []
                    for p in range(2):
                        n = 4 * s + 2 * h + p
                        car = car_ref[n] if diagonal else car_ref[n] + spent
                        w = jnp.exp(zl_ref[n] + cs_ref[n * T:(n + 1) * T, :T] + car)
                        if diagonal:
                            w = jnp.where(valid, w, 0.0)
                        w_cols.append(w.astype(BF16))
                        car = car + cs_ref[n * T:(n + 1) * T, T:]
                        car_ref[n] = car
                        top = car if top is None else jnp.maximum(top, car)
                    w_rows.append(jnp.concatenate(w_cols, axis=1))
                acc_ref[s] += _dot(jnp.concatenate(w_rows, axis=0), vb_ref[s])
        return jnp.max(top)

    def cond(carry):
        m, alive = carry
        return jnp.logical_and(m <= SB_QB * i + SB_QB - 1, alive > SB_DEAD)

    def body(carry):
        m, _ = carry
        return m + 1, sweep(m, diagonal=False)

    lax.while_loop(cond, body, (jnp.int32(1), sweep(0, diagonal=True)))

    head_of_lane = lane // SB_DH
    for a in range(SB_QB):
        for g in range(SB_GROUPS):
            acc = acc_ref[a * SB_GROUPS + g]
            o = jnp.where(is_h0, acc[:T], acc[T:])
            sq = o * o
            ms = jnp.zeros_like(o)
            for hd in range(SB_GW // SB_DH):
                mine = head_of_lane == hd
                ms = jnp.where(mine, jnp.sum(jnp.where(mine, sq, 0.0), axis=-1, keepdims=True), ms)
            cols = slice(g * SB_GW, (g + 1) * SB_GW)
            o_ref[a * T:(a + 1) * T, cols] = (o * lax.rsqrt(ms * (1.0 / SB_DH) + EPS) * gn_ref[:, cols]).astype(o_ref.dtype)


def _sb(qs, ks, vs, tu, gn, B, S):
    rows = SB_QB * SB_T
    n_q = S // rows
    width = SB_HEADS * SB_DH
    n_slab = SB_QB * SB_GROUPS
    return pl.pallas_call(
        _sb_kernel,
        grid=(B, n_q),
        in_specs=[pl.BlockSpec((rows, width), lambda b, i: (b * n_q + i, 0)),
                  pl.BlockSpec((S, width), lambda b, i: (b, 0)),
                  pl.BlockSpec((S, width), lambda b, i: (b, 0)),
                  pl.BlockSpec((2 * SB_T, 2 * SB_T), lambda b, i: (0, 0)),
                  pl.BlockSpec((1, width), lambda b, i: (0, 0))],
        out_specs=pl.BlockSpec((rows, width), lambda b, i: (b * n_q + i, 0)),
        out_shape=jax.ShapeDtypeStruct((B * S, width), BF16),
        scratch_shapes=[pltpu.VMEM((n_slab, 2 * SB_T, SB_GW), F32),
                        pltpu.VMEM((SB_TILES, SB_T, SB_T), F32),
                        pltpu.VMEM((n_slab, 2 * SB_T, SB_GW), BF16),
                        pltpu.VMEM((n_slab, 2 * SB_T, SB_GW), BF16),
                        pltpu.VMEM((SB_TILES, SB_T, SB_T), F32),
                        pltpu.VMEM((SB_TILES * SB_T, 2 * SB_T), BF16),
                        pltpu.VMEM((SB_TILES * SB_T, 2 * SB_T), F32)],
        compiler_params=pltpu.CompilerParams(dimension_semantics=("arbitrary", "arbitrary"),
                                             vmem_limit_bytes=VMEM_LIMIT),
        name="sb",
    )(qs, ks, vs, tu, gn)


def _layer_norm(y, g, b):
    mu = jnp.mean(y, axis=-1, keepdims=True)
    d = y - mu
    var = jnp.mean(d * d, axis=-1, keepdims=True)
    return d * lax.rsqrt(var + EPS) * g + b


def _outproj_kernel(og_ref, os_ref, x_ref, wo_ref, g_ref, b_ref, x1_ref, x1b_ref):
    half = og_ref.shape[1]
    mix = _dot(og_ref[...], wo_ref[:half, :]) + _dot(os_ref[...], wo_ref[half:, :])
    x1 = _layer_norm(ALPHA * x_ref[...] + mix, g_ref[...], b_ref[...])
    x1_ref[...] = x1
    x1b_ref[...] = x1.astype(BF16)


def _outproj(ogla, osb, x2d, wo, g, b, tm):
    T, D = x2d.shape
    half = ogla.shape[1]
    return pl.pallas_call(
        _outproj_kernel,
        grid=(T // tm,),
        in_specs=[pl.BlockSpec((tm, half), lambda i: (i, 0)),
                  pl.BlockSpec((tm, half), lambda i: (i, 0)),
                  pl.BlockSpec((tm, D), lambda i: (i, 0)),
                  pl.BlockSpec((2 * half, D), lambda i: (0, 0)),
                  pl.BlockSpec((1, D), lambda i: (0, 0)),
                  pl.BlockSpec((1, D), lambda i: (0, 0))],
        out_specs=[pl.BlockSpec((tm, D), lambda i: (i, 0)),
                   pl.BlockSpec((tm, D), lambda i: (i, 0))],
        out_shape=[jax.ShapeDtypeStruct((T, D), F32), jax.ShapeDtypeStruct((T, D), BF16)],
        compiler_params=pltpu.CompilerParams(dimension_semantics=("arbitrary",),
                                             vmem_limit_bytes=VMEM_LIMIT),
        name="outproj",
    )(ogla, osb, x2d, wo, g, b)


PEER_SEL_HEADS_PER_BODY = 4
SUBLANES = 8
PACKED_ROWS = 16


def _batcher_network(n):
    pairs, p = [], 1
    while p < n:
        k = p
        while k >= 1:
            for j in range(k % p, n - k, 2 * k):
                for i in range(min(k, n - j - k)):
                    if (i + j) // (2 * p) == (i + j + k) // (2 * p):
                        pairs.append((i + j, i + j + k))
            k //= 2
        p *= 2
    return pairs


def _merge_top(stacks, n):
    stacks = [list(st) for st in stacks]
    tops = []
    for r in range(n):
        head = stacks[0][0]
        for st in stacks[1:]:
            head = jnp.maximum(head, st[0])
        m = jnp.max(head, axis=0, keepdims=True)
        tops.append(m)
        need = n - r - 1
        for st in stacks:
            if need == 0:
                break
            hit = st[0] == m
            depth = min(need, len(st))
            for d in range(depth):
                below = st[d + 1] if d + 1 < len(st) else NEG_INF
                st[d] = jnp.where(hit, below, st[d])
            del st[depth:]
    return tops


def _top_rows(s, n):
    slabs = [s[r:r + SUBLANES] for r in range(0, s.shape[0], SUBLANES)]
    for i, j in _batcher_network(len(slabs)):
        slabs[i], slabs[j] = jnp.maximum(slabs[i], slabs[j]), jnp.minimum(slabs[i], slabs[j])
    return _merge_top([slabs], n)


def _peer_sel_kernel(x_ref, wq_ref, sk_ref, s1_ref, e1_ref, th_ref, c_ref, q_scr):
    tt = x_ref.shape[0]
    qry = _dot(x_ref[...], wq_ref[...]).astype(BF16)
    for hp in range(2 * PEER_HEADS):
        q_scr[hp] = qry[:, hp * PEER_HALF:(hp + 1) * PEER_HALF]
    sub = lax.broadcasted_iota(jnp.int32, (SUBLANES, tt), 0)

    def head(h):
        s0 = _dot_nt(sk_ref[2 * h], q_scr[2 * h])
        s1 = _dot_nt(sk_ref[2 * h + 1], q_scr[2 * h + 1])
        n = PEER_TOPK + 1
        a = _top_rows(s0, n)
        b = _top_rows(s1, n)
        a_lo = jnp.concatenate(a[:SUBLANES], axis=0)
        a_hi = jnp.concatenate(a[SUBLANES:2 * SUBLANES], axis=0)
        xs = [jnp.where(sub < n // l, a_lo + b[l - 1], NEG_INF) for l in range(1, n + 1)]
        zs = [a_hi + b[0], jnp.where(sub < 1, a[n - 1] + b[0], NEG_INF)]
        best = _merge_top([xs, zs], n)
        z = jnp.zeros_like(best[0])
        for r in range(PEER_TOPK):
            z = z + jnp.exp(best[r] - best[0])
        tau = 0.5 * (best[PEER_TOPK - 1] + best[PEER_TOPK])
        d1 = s1 - b[0]
        d0 = s0 - a[0]
        tau_rel = tau - best[0]
        scale = PEER_QONE / jnp.maximum(-tau_rel, 1e-30)
        s1_ref[0, h] = jnp.maximum(d1 * scale, -2.0 * PEER_QONE).astype(jnp.int32).astype(jnp.int16)
        th_ref[0, h] = jnp.minimum((tau_rel - d0) * scale, 2.0 * PEER_QONE - 1.0).astype(jnp.int32)
        e1_ref[0, h] = jnp.exp(d1).astype(BF16)
        c_ref[0, h] = jnp.exp(d0) * (0.5 / z)

    def head_group(i, _):
        for n in range(PEER_SEL_HEADS_PER_BODY):
            head(PEER_SEL_HEADS_PER_BODY * i + n)
        return 0

    lax.fori_loop(0, PEER_HEADS // PEER_SEL_HEADS_PER_BODY, head_group, 0)


def _peer_sel(x1b, wq, sk, tt):
    T, D = x1b.shape
    qd = wq.shape[1]
    shp = [jax.ShapeDtypeStruct((T // tt, PEER_HEADS, PEER_NKEYS, tt), dt)
           for dt in (jnp.int16, BF16, jnp.int32, F32)]
    ospec = pl.BlockSpec((1, PEER_HEADS, PEER_NKEYS, tt), lambda i: (i, 0, 0, 0))
    return pl.pallas_call(
        _peer_sel_kernel,
        grid=(T // tt,),
        in_specs=[pl.BlockSpec((tt, D), lambda i: (i, 0)),
                  pl.BlockSpec((D, qd), lambda i: (0, 0)),
                  pl.BlockSpec((2 * PEER_HEADS, PEER_NKEYS, PEER_HALF), lambda i: (0, 0, 0))],
        out_specs=[ospec, ospec, ospec, ospec],
        out_shape=shp,
        scratch_shapes=[pltpu.VMEM((2 * PEER_HEADS, tt, PEER_HALF), BF16)],
        compiler_params=pltpu.CompilerParams(dimension_semantics=("arbitrary",),
                                             vmem_limit_bytes=VMEM_LIMIT),
        name="peer_sel",
    )(x1b, wq, sk)


PEER_ET = 2048
PEER_EQ = 512
PEER_LC = 256


def _peer_ffn_kernel(xt_ref, x1_ref, u_ref, vt_ref, s1_ref, e1_ref, th_ref, c_ref, g_ref, b_ref,
                     o_ref, acc_ref, h_ref, p_ref):
    e = pl.program_id(1)
    n_lc = xt_ref.shape[0]
    n_q = PEER_ET // PEER_EQ
    n_k = PEER_EQ // PEER_NKEYS
    assert n_lc == 2

    @pl.when(e == 0)
    def _():
        acc_ref[...] = jnp.zeros_like(acc_ref)

    def scores(q, lc):
        r0 = q * PEER_EQ if isinstance(q, int) else pl.multiple_of(q * PEER_EQ, PEER_EQ)
        h_ref[lc] = _dot(u_ref[pl.ds(r0, PEER_EQ), :], xt_ref[lc])

    def gated_act(q, lc):
        reps = PEER_NKEYS // PACKED_ROWS
        for k in range(n_k):
            rows = slice(k * PEER_NKEYS, (k + 1) * PEER_NKEYS)
            gate = jnp.zeros((PEER_NKEYS, PEER_LC), BF16)
            for h in range(PEER_HEADS):
                th = th_ref[lc, h, pl.ds(q * n_k + k, 1), :]
                cc = c_ref[lc, h, pl.ds(q * n_k + k, 1), :]
                th = jnp.tile(jnp.broadcast_to(th, (PACKED_ROWS, PEER_LC)).astype(jnp.int16), (reps, 1))
                cc = jnp.tile(jnp.broadcast_to(cc, (PACKED_ROWS, PEER_LC)).astype(BF16), (reps, 1))
                val = e1_ref[lc, h] * cc
                gate = gate + jnp.where(s1_ref[lc, h] >= th, val, jnp.zeros_like(val))
            hh = h_ref[lc, rows, :]
            act = hh + hh * lax.erf(hh * (2.0 ** -0.5))
            p_ref[lc, rows, :] = gate * act.astype(BF16)

    def accumulate(q, lc):
        acc_ref[lc] += _dot(vt_ref[q], p_ref[lc])

    scores(0, 0)
    scores(0, 1)
    gated_act(0, 0)

    def step(q, _):
        scores(q + 1, 0)
        gated_act(q, 1)
        accumulate(q, 0)
        scores(q + 1, 1)
        gated_act(q + 1, 0)
        accumulate(q, 1)
        return 0

    lax.fori_loop(0, n_q - 1, step, 0)
    gated_act(n_q - 1, 1)
    accumulate(n_q - 1, 0)
    accumulate(n_q - 1, 1)

    @pl.when(e == pl.num_programs(1) - 1)
    def _():
        for lc in range(n_lc):
            rows = slice(lc * PEER_LC, (lc + 1) * PEER_LC)
            y = ALPHA * x1_ref[rows, :] + acc_ref[lc].T
            o_ref[rows, :] = _layer_norm(y, g_ref[...], b_ref[...])


def _peer_ffn(xt, x1, u, vt, s1, e1, th, cc, g, b, tt):
    T, D = x1.shape
    E = u.shape[0]
    n_lc = tt // PEER_LC
    n_q = PEER_ET // PEER_EQ
    tok = pl.BlockSpec((n_lc, PEER_HEADS, PEER_NKEYS, PEER_LC), lambda t, e: (t, 0, 0, 0))
    sel = pl.BlockSpec((n_lc, PEER_HEADS, PEER_ET // PEER_NKEYS, PEER_LC), lambda t, e: (t, 0, e, 0))
    return pl.pallas_call(
        _peer_ffn_kernel,
        grid=(T // tt, E // PEER_ET),
        in_specs=[pl.BlockSpec((n_lc, D, PEER_LC), lambda t, e: (t, 0, 0)),
                  pl.BlockSpec((tt, D), lambda t, e: (t, 0)),
                  pl.BlockSpec((PEER_ET, D), lambda t, e: (e, 0)),
                  pl.BlockSpec((n_q, D, PEER_EQ), lambda t, e: (e, 0, 0)),
                  tok, tok, sel, sel,
                  pl.BlockSpec((1, D), lambda t, e: (0, 0)),
                  pl.BlockSpec((1, D), lambda t, e: (0, 0))],
        out_specs=pl.BlockSpec((tt, D), lambda t, e: (t, 0)),
        out_shape=jax.ShapeDtypeStruct((T, D), F32),
        scratch_shapes=[pltpu.VMEM((n_lc, D, PEER_LC), F32),
                        pltpu.VMEM((n_lc, PEER_EQ, PEER_LC), F32),
                        pltpu.VMEM((n_lc, PEER_EQ, PEER_LC), BF16)],
        compiler_params=pltpu.CompilerParams(dimension_semantics=("arbitrary", "arbitrary"),
                                             vmem_limit_bytes=VMEM_LIMIT),
        name="peer_ffn",
    )(xt, x1, u, vt, s1, e1, th, cc, g, b)


def _layer(x, w_in, w_gla_gate, b_gla_gate, gla_norm_g, sb_norm_g, w_out, ln1_g, ln1_b,
           peer_w_query, peer_sub_keys, peer_u, peer_v, ln2_g, ln2_b):
    B, S, D = x.shape
    T = B * S
    x2d = x.reshape(T, D)

    kw, gw, sw = GLA_HEADS * GLA_DK, GLA_HEADS * GLA_DV, SB_HEADS * SB_DH
    pts = [0, kw, 2 * kw, 2 * kw + gw, 2 * kw + 2 * gw, 2 * kw + 2 * gw + GLA_RANK]
    pts += [pts[-1] + sw, pts[-1] + 2 * sw, pts[-1] + 3 * sw]
    cols = [w_in[:, pts[n]:pts[n + 1]] for n in range(8)]
    cols[4] = jnp.pad(cols[4], ((0, 0), (0, LANES - GLA_RANK)))
    w_cat = jnp.concatenate(cols, axis=1).astype(BF16)
    wg = jnp.pad(w_gla_gate, ((0, LANES - GLA_RANK), (0, 0))).astype(BF16)

    qg, kg, vg, gg, alr, qs, ks, vs = _inproj(x2d, w_cat, min(TOKEN_TILE, T))

    ogla = _gla(qg, kg, vg, gg, alr, wg, b_gla_gate.reshape(1, -1), gla_norm_g.reshape(1, -1),
                B, S, min(GLA_SEQ_BLOCK, S))

    r = lax.broadcasted_iota(jnp.int32, (2 * SB_T, 2 * SB_T), 0) % SB_T
    c = lax.broadcasted_iota(jnp.int32, (2 * SB_T, 2 * SB_T), 1)
    tu = jnp.where((c >= SB_T) | (r > c), 1.0, 0.0).astype(BF16)
    osb = _sb(qs, ks, vs, tu, sb_norm_g.reshape(1, -1), B, S)

    x1, x1b = _outproj(ogla, osb, x2d, w_out.astype(BF16), ln1_g.reshape(1, D), ln1_b.reshape(1, D),
                       min(TOKEN_TILE, T))

    sk = peer_sub_keys.reshape(2 * PEER_HEADS, PEER_NKEYS, PEER_HALF).astype(BF16)
    s1, e1, th, cc = _peer_sel(x1b, peer_w_query.astype(BF16), sk, PEER_LC)

    xt = x1b.reshape(-1, PEER_LC, D).transpose(0, 2, 1)
    vt = peer_v.reshape(-1, PEER_EQ, D).transpose(0, 2, 1).astype(BF16)
    out = _peer_ffn(xt, x1, peer_u.astype(BF16), vt, s1, e1, th, cc,
                    ln2_g.reshape(1, D), ln2_b.reshape(1, D), 2 * PEER_LC)
    return out.reshape(B, S, D)


def kernel(x, w_in, w_gla_gate, b_gla_gate, gla_norm_g, sb_norm_g, w_out, ln1_g, ln1_b,
           peer_w_query, peer_sub_keys, peer_u, peer_v, ln2_g, ln2_b):
    for l in range(DEPTH):
        x = _layer(x, w_in[l], w_gla_gate[l], b_gla_gate[l], gla_norm_g[l], sb_norm_g[l], w_out[l],
                   ln1_g[l], ln1_b[l], peer_w_query[l], peer_sub_keys[l], peer_u[l], peer_v[l],
                   ln2_g[l], ln2_b[l])
    return x
```

```python
import functools

import jax
import jax.numpy as jnp
from jax import lax
from jax.experimental import pallas as pl
from jax.experimental.pallas import tpu as pltpu

F32 = jnp.float32
BF16 = jnp.bfloat16

LANES = 128
GLA_HEADS = 4
GLA_DK = 64
GLA_DV = 128
GLA_RANK = 16
GLA_GATE_TEMP = 16.0
GLA_CHUNK = 64
SB_HEADS = 8
SB_DH = 64
PEER_HEADS = 8
PEER_NKEYS = 128
PEER_HALF = 128
PEER_TOPK = 16
PEER_QONE = 16384.0
DEPTH = 1
ALPHA = (2.0 * DEPTH) ** 0.25
EPS = 1e-5
NEG_INF = float("-inf")
SB_DEAD = -104.0

VMEM_CAPACITY_V7X = 64 * 1024 * 1024
VMEM_LIMIT = VMEM_CAPACITY_V7X * 7 // 8
TOKEN_TILE = 512
GLA_SEQ_BLOCK = 512


def _dot(a, b):
    return jnp.dot(a, b, preferred_element_type=F32)


def _dot_nt(a, b):
    return lax.dot_general(a, b, (((1,), (1,)), ((), ())), preferred_element_type=F32)


def _dot_tn(a, b):
    return lax.dot_general(a, b, (((0,), (0,)), ((), ())), preferred_element_type=F32)


def _split_bf16(v):
    hi = v.astype(BF16)
    lo = (v - hi.astype(F32)).astype(BF16)
    return hi, lo


def _log_sigmoid(z):
    return jnp.minimum(z, 0.0) - jnp.log(1.0 + jnp.exp(-jnp.abs(z)))


IN_SPLITS = (("qg", 256, F32), ("kg", 256, F32), ("vg", 512, BF16), ("gg", 512, F32),
             ("alr", LANES, F32), ("qs", 512, BF16), ("ks", 512, BF16), ("vs", 512, BF16))


def _inproj_kernel(x_ref, w_ref, *out_refs):
    xb = x_ref[...].astype(BF16)
    off = 0
    for (name, width, _), o_ref in zip(IN_SPLITS, out_refs):
        r = _dot(xb, w_ref[:, off:off + width])
        if name in ("qg", "qs"):
            r = r * (GLA_DK ** -0.5)
        o_ref[...] = r.astype(o_ref.dtype)
        off += width


def _inproj(x2d, w_cat, tm):
    T, D = x2d.shape
    wtot = w_cat.shape[1]
    return pl.pallas_call(
        _inproj_kernel,
        grid=(T // tm,),
        in_specs=[pl.BlockSpec((tm, D), lambda i: (i, 0)),
                  pl.BlockSpec((D, wtot), lambda i: (0, 0))],
        out_specs=[pl.BlockSpec((tm, w), lambda i: (i, 0)) for _, w, _ in IN_SPLITS],
        out_shape=[jax.ShapeDtypeStruct((T, w), dt) for _, w, dt in IN_SPLITS],
        compiler_params=pltpu.CompilerParams(dimension_semantics=("arbitrary",),
                                             vmem_limit_bytes=VMEM_LIMIT),
        name="inproj",
    )(x2d, w_cat)


def _gla_kernel(q_ref, k_ref, v_ref, g_ref, a_ref, wg_ref, bg_ref, gn_ref, o_ref, st_ref, *, n_chunks):
    @pl.when(pl.program_id(0) == 0)
    def _():
        st_ref[...] = jnp.zeros_like(st_ref)

    C = GLA_CHUNK
    KW, VW = 2 * GLA_DK, 2 * GLA_DV
    lane = lax.broadcasted_iota(jnp.int32, (1, KW), 1)
    head_mask = [lane < GLA_DK, lane >= GLA_DK]
    r_i = lax.broadcasted_iota(jnp.int32, (C, C), 0)
    c_i = lax.broadcasted_iota(jnp.int32, (C, C), 1)
    causal = c_i <= r_i
    tri = jnp.where(causal, 1.0, 0.0).astype(BF16)

    n_batch = q_ref.shape[0]
    la_all = [_log_sigmoid(_dot(a_ref[bi].astype(BF16), wg_ref[...]) + bg_ref[...]) * (1.0 / GLA_GATE_TEMP)
              for bi in range(n_batch)]

    for c in range(n_chunks):
        rows = slice(c * C, (c + 1) * C)
        for bi in range(n_batch):
            la = la_all[bi][rows]
            hi, lo = _split_bf16(la)
            b_all = _dot(tri, hi) + _dot(tri, lo)
            for hp in range(GLA_HEADS // 2):
                kcols = slice(hp * KW, (hp + 1) * KW)
                vcols = slice(hp * VW, (hp + 1) * VW)
                b = b_all[:, kcols]
                b_last = b[C - 1:C, :]
                q = q_ref[bi, rows, kcols]
                k = k_ref[bi, rows, kcols]
                q_in = q * jnp.exp(b)
                k_in = (k * jnp.exp(-b)).astype(BF16)
                k_st = (k * jnp.exp(b_last - b)).astype(BF16)
                dec = jnp.exp(b_last)
                v = v_ref[bi, rows, vcols]
                st = st_ref[bi, hp]
                st_b = st.astype(BF16)
                outs = []
                for h in range(2):
                    qm = jnp.where(head_mask[h], q_in, 0.0).astype(BF16)
                    att = jnp.where(causal, _dot_nt(qm, k_in), 0.0)
                    vh = v[:, h * GLA_DV:(h + 1) * GLA_DV]
                    o_h = _dot(att.astype(BF16), vh) + _dot_nt(qm, st_b[h * GLA_DV:(h + 1) * GLA_DV, :])
                    ms = jnp.mean(o_h * o_h, axis=-1, keepdims=True)
                    outs.append(o_h * lax.rsqrt(ms + EPS))
                st_ref[bi, hp] = st * dec + _dot_tn(v, k_st)
                o = jnp.concatenate(outs, axis=1) * gn_ref[:, vcols]
                g = g_ref[bi, rows, vcols]
                o_ref[bi, rows, vcols] = (o * (g * jax.nn.sigmoid(g))).astype(o_ref.dtype)


def _gla(qg, kg, vg, gg, alr, wg, bg, gn, B, S, ls):
    kw, vw = GLA_HEADS * GLA_DK, GLA_HEADS * GLA_DV
    seq = lambda s: (0, s, 0)
    par = lambda s: (0, 0)
    per_batch = lambda t: t.reshape(B, S, t.shape[-1])
    out = pl.pallas_call(
        functools.partial(_gla_kernel, n_chunks=ls // GLA_CHUNK),
        grid=(S // ls,),
        in_specs=[pl.BlockSpec((B, ls, kw), seq),
                  pl.BlockSpec((B, ls, kw), seq),
                  pl.BlockSpec((B, ls, vw), seq),
                  pl.BlockSpec((B, ls, vw), seq),
                  pl.BlockSpec((B, ls, LANES), seq),
                  pl.BlockSpec((LANES, kw), par),
                  pl.BlockSpec((1, kw), par),
                  pl.BlockSpec((1, vw), par)],
        out_specs=pl.BlockSpec((B, ls, vw), seq),
        out_shape=jax.ShapeDtypeStruct((B, S, vw), BF16),
        scratch_shapes=[pltpu.VMEM((B, GLA_HEADS // 2, 2 * GLA_DV, 2 * GLA_DK), F32)],
        compiler_params=pltpu.CompilerParams(dimension_semantics=("arbitrary",),
                                             vmem_limit_bytes=VMEM_LIMIT),
        name="gla",
    )(per_batch(qg), per_batch(kg), per_batch(vg), per_batch(gg), per_batch(alr), wg, bg, gn)
    return out.reshape(B * S, vw)


SB_T = 128
SB_QB = 4
SB_GW = 2 * LANES
SB_GROUPS = SB_HEADS * SB_DH // SB_GW
SB_TILES = 4 * SB_GROUPS * SB_QB


def _sb_kernel(q_ref, k_ref, v_ref, tu_ref, gn_ref, o_ref,
               acc_ref, car_ref, kb_ref, vb_ref, zl_ref, l_ref, cs_ref):
    i = pl.program_id(1)
    T = SB_T
    lane = lax.broadcasted_iota(jnp.int32, (1, SB_GW), 1)
    is_h0 = (lane % LANES) < SB_DH
    qg = []
    for a in range(SB_QB):
        qa = []
        for g in range(SB_GROUPS):
            q = q_ref[a * T:(a + 1) * T, g * SB_GW:(g + 1) * SB_GW]
            zero = jnp.zeros_like(q)
            qa.append(jnp.concatenate([jnp.where(is_h0, q, zero), jnp.where(is_h0, zero, q)], axis=0))
        qg.append(qa)
    row = lax.broadcasted_iota(jnp.int32, (T, T), 0)
    col = lax.broadcasted_iota(jnp.int32, (T, T), 1)
    acc_ref[...] = jnp.zeros_like(acc_ref)
    car_ref[...] = jnp.zeros_like(car_ref)
    kb_ref[...] = jnp.zeros_like(kb_ref)
    vb_ref[...] = jnp.zeros_like(vb_ref)

    def sweep(m, diagonal):
        valid = col < row
        for a in range(SB_QB):
            j = SB_QB * i + a - m
            ks = pl.multiple_of(jnp.maximum(j, 0) * T, T)
            for g in range(SB_GROUPS):
                s = a * SB_GROUPS + g
                for p in range(2):
                    src = slice(g * SB_GW + p * LANES, g * SB_GW + (p + 1) * LANES)
                    kb_ref[s, p * T:(p + 1) * T, p * LANES:(p + 1) * LANES] = k_ref[pl.ds(ks, T), src]
                    vb_ref[s, p * T:(p + 1) * T, p * LANES:(p + 1) * LANES] = v_ref[pl.ds(ks, T), src]
                z = _dot_nt(qg[a][g], kb_ref[s])
                for h in range(2):
                    for p in range(2):
                        n = 4 * s + 2 * h + p
                        zq = z[h * T:(h + 1) * T, p * T:(p + 1) * T]
                        lnb = -(jnp.maximum(zq, 0.0) + jnp.log(1.0 + jnp.exp(-jnp.abs(zq))))
                        if diagonal:
                            lnb = jnp.where(valid, lnb, 0.0)
                        hi, lo = _split_bf16(lnb)
                        l_ref[n * T:(n + 1) * T, :T] = hi
                        l_ref[n * T:(n + 1) * T, T:] = lo
                        zl_ref[n] = zq + lnb
        cs_ref[...] = _dot(l_ref[...], tu_ref[...])
        top = None
        for a in range(SB_QB):
            j = SB_QB * i + a - m
            spent = jnp.where(j < 0, -1e30, 0.0).astype(F32)
            for g in range(SB_GROUPS):
                s = a * SB_GROUPS + g
                w_rows = []
                for h in range(2):
                    w_cols = []
                    for p in range(2):
                        n = 4 * s + 2 * h + p
                        car = car_ref[n] if diagonal else car_ref[n] + spent
                        w = jnp.exp(zl_ref[n] + cs_ref[n * T:(n + 1) * T, :T] + car)
                        if diagonal:
                            w = jnp.where(valid, w, 0.0)
                        w_cols.append(w.astype(BF16))
                        car = car + cs_ref[n * T:(n + 1) * T, T:]
                        car_ref[n] = car
                        top = car if top is None else jnp.maximum(top, car)
                    w_rows.append(jnp.concatenate(w_cols, axis=1))
                acc_ref[s] += _dot(jnp.concatenate(w_rows, axis=0), vb_ref[s])
        return jnp.max(top)

    def cond(carry):
        m, alive = carry
        return jnp.logical_and(m <= SB_QB * i + SB_QB - 1, alive > SB_DEAD)

    def body(carry):
        m, _ = carry
        return m + 1, sweep(m, diagonal=False)

    lax.while_loop(cond, body, (jnp.int32(1), sweep(0, diagonal=True)))

    head_of_lane = lane // SB_DH
    for a in range(SB_QB):
        for g in range(SB_GROUPS):
            acc = acc_ref[a * SB_GROUPS + g]
            o = jnp.where(is_h0, acc[:T], acc[T:])
            sq = o * o
            ms = jnp.zeros_like(o)
            for hd in range(SB_GW // SB_DH):
                mine = head_of_lane == hd
                ms = jnp.where(mine, jnp.sum(jnp.where(mine, sq, 0.0), axis=-1, keepdims=True), ms)
            cols = slice(g * SB_GW, (g + 1) * SB_GW)
            o_ref[a * T:(a + 1) * T, cols] = (o * lax.rsqrt(ms * (1.0 / SB_DH) + EPS) * gn_ref[:, cols]).astype(o_ref.dtype)


def _sb(qs, ks, vs, tu, gn, B, S):
    rows = SB_QB * SB_T
    n_q = S // rows
    width = SB_HEADS * SB_DH
    n_slab = SB_QB * SB_GROUPS
    return pl.pallas_call(
        _sb_kernel,
        grid=(B, n_q),
        in_specs=[pl.BlockSpec((rows, width), lambda b, i: (b * n_q + i, 0)),
                  pl.BlockSpec((S, width), lambda b, i: (b, 0)),
                  pl.BlockSpec((S, width), lambda b, i: (b, 0)),
                  pl.BlockSpec((2 * SB_T, 2 * SB_T), lambda b, i: (0, 0)),
                  pl.BlockSpec((1, width), lambda b, i: (0, 0))],
        out_specs=pl.BlockSpec((rows, width), lambda b, i: (b * n_q + i, 0)),
        out_shape=jax.ShapeDtypeStruct((B * S, width), BF16),
        scratch_shapes=[pltpu.VMEM((n_slab, 2 * SB_T, SB_GW), F32),
                        pltpu.VMEM((SB_TILES, SB_T, SB_T), F32),
                        pltpu.VMEM((n_slab, 2 * SB_T, SB_GW), BF16),
                        pltpu.VMEM((n_slab, 2 * SB_T, SB_GW), BF16),
                        pltpu.VMEM((SB_TILES, SB_T, SB_T), F32),
                        pltpu.VMEM((SB_TILES * SB_T, 2 * SB_T), BF16),
                        pltpu.VMEM((SB_TILES * SB_T, 2 * SB_T), F32)],
        compiler_params=pltpu.CompilerParams(dimension_semantics=("arbitrary", "arbitrary"),
                                             vmem_limit_bytes=VMEM_LIMIT),
        name="sb",
    )(qs, ks, vs, tu, gn)


def _layer_norm(y, g, b):
    mu = jnp.mean(y, axis=-1, keepdims=True)
    d = y - mu
    var = jnp.mean(d * d, axis=-1, keepdims=True)
    return d * lax.rsqrt(var + EPS) * g + b


def _outproj_kernel(og_ref, os_ref, x_ref, wo_ref, g_ref, b_ref, x1_ref, x1b_ref):
    half = og_ref.shape[1]
    mix = _dot(og_ref[...], wo_ref[:half, :]) + _dot(os_ref[...], wo_ref[half:, :])
    x1 = _layer_norm(ALPHA * x_ref[...] + mix, g_ref[...], b_ref[...])
    x1_ref[...] = x1
    x1b_ref[...] = x1.astype(BF16)


def _outproj(ogla, osb, x2d, wo, g, b, tm):
    T, D = x2d.shape
    half = ogla.shape[1]
    return pl.pallas_call(
        _outproj_kernel,
        grid=(T // tm,),
        in_specs=[pl.BlockSpec((tm, half), lambda i: (i, 0)),
                  pl.BlockSpec((tm, half), lambda i: (i, 0)),
                  pl.BlockSpec((tm, D), lambda i: (i, 0)),
                  pl.BlockSpec((2 * half, D), lambda i: (0, 0)),
                  pl.BlockSpec((1, D), lambda i: (0, 0)),
                  pl.BlockSpec((1, D), lambda i: (0, 0))],
        out_specs=[pl.BlockSpec((tm, D), lambda i: (i, 0)),
                   pl.BlockSpec((tm, D), lambda i: (i, 0))],
        out_shape=[jax.ShapeDtypeStruct((T, D), F32), jax.ShapeDtypeStruct((T, D), BF16)],
        compiler_params=pltpu.CompilerParams(dimension_semantics=("arbitrary",),
                                             vmem_limit_bytes=VMEM_LIMIT),
        name="outproj",
    )(ogla, osb, x2d, wo, g, b)


PEER_SEL_HEADS_PER_BODY = 4
SUBLANES = 8
PACKED_ROWS = 16


def _batcher_network(n):
    pairs, p = [], 1
    while p < n:
        k = p
        while k >= 1:
            for j in range(k % p, n - k, 2 * k):
                for i in range(min(k, n - j - k)):
                    if (i + j) // (2 * p) == (i + j + k) // (2 * p):
                        pairs.append((i + j, i + j + k))
            k //= 2
        p *= 2
    return pairs


def _merge_top(stacks, n):
    stacks = [list(st) for st in stacks]
    tops = []
    for r in range(n):
        head = stacks[0][0]
        for st in stacks[1:]:
            head = jnp.maximum(head, st[0])
        m = jnp.max(head, axis=0, keepdims=True)
        tops.append(m)
        need = n - r - 1
        for st in stacks:
            if need == 0:
                break
            hit = st[0] == m
            depth = min(need, len(st))
            for d in range(depth):
                below = st[d + 1] if d + 1 < len(st) else NEG_INF
                st[d] = jnp.where(hit, below, st[d])
            del st[depth:]
    return tops


def _top_rows(s, n):
    slabs = [s[r:r + SUBLANES] for r in range(0, s.shape[0], SUBLANES)]
    for i, j in _batcher_network(len(slabs)):
        slabs[i], slabs[j] = jnp.maximum(slabs[i], slabs[j]), jnp.minimum(slabs[i], slabs[j])
    return _merge_top([slabs], n)


def _peer_sel_kernel(x_ref, wq_ref, sk_ref, s1_ref, e1_ref, th_ref, c_ref, q_scr):
    tt = x_ref.shape[0]
    qry = _dot(x_ref[...], wq_ref[...]).astype(BF16)
    for hp in range(2 * PEER_HEADS):
        q_scr[hp] = qry[:, hp * PEER_HALF:(hp + 1) * PEER_HALF]
    sub = lax.broadcasted_iota(jnp.int32, (SUBLANES, tt), 0)

    def head(h):
        s0 = _dot_nt(sk_ref[2 * h], q_scr[2 * h])
        s1 = _dot_nt(sk_ref[2 * h + 1], q_scr[2 * h + 1])
        n = PEER_TOPK + 1
        a = _top_rows(s0, n)
        b = _top_rows(s1, n)
        a_lo = jnp.concatenate(a[:SUBLANES], axis=0)
        a_hi = jnp.concatenate(a[SUBLANES:2 * SUBLANES], axis=0)
        xs = [jnp.where(sub < n // l, a_lo + b[l - 1], NEG_INF) for l in range(1, n + 1)]
        zs = [a_hi + b[0], jnp.where(sub < 1, a[n - 1] + b[0], NEG_INF)]
        best = _merge_top([xs, zs], n)
        z = jnp.zeros_like(best[0])
        for r in range(PEER_TOPK):
            z = z + jnp.exp(best[r] - best[0])
        tau = 0.5 * (best[PEER_TOPK - 1] + best[PEER_TOPK])
        d1 = s1 - b[0]
        d0 = s0 - a[0]
        tau_rel = tau - best[0]
        scale = PEER_QONE / jnp.maximum(-tau_rel, 1e-30)
        s1_ref[0, h] = jnp.maximum(d1 * scale, -2.0 * PEER_QONE).astype(jnp.int32).astype(jnp.int16)
        th_ref[0, h] = jnp.minimum((tau_rel - d0) * scale, 2.0 * PEER_QONE - 1.0).astype(jnp.int32)
        e1_ref[0, h] = jnp.exp(d1).astype(BF16)
        c_ref[0, h] = jnp.exp(d0) * (0.5 / z)

    def head_group(i, _):
        for n in range(PEER_SEL_HEADS_PER_BODY):
            head(PEER_SEL_HEADS_PER_BODY * i + n)
        return 0

    lax.fori_loop(0, PEER_HEADS // PEER_SEL_HEADS_PER_BODY, head_group, 0)


def _peer_sel(x1b, wq, sk, tt):
    T, D = x1b.shape
    qd = wq.shape[1]
    shp = [jax.ShapeDtypeStruct((T // tt, PEER_HEADS, PEER_NKEYS, tt), dt)
           for dt in (jnp.int16, BF16, jnp.int32, F32)]
    ospec = pl.BlockSpec((1, PEER_HEADS, PEER_NKEYS, tt), lambda i: (i, 0, 0, 0))
    return pl.pallas_call(
        _peer_sel_kernel,
        grid=(T // tt,),
        in_specs=[pl.BlockSpec((tt, D), lambda i: (i, 0)),
                  pl.BlockSpec((D, qd), lambda i: (0, 0)),
                  pl.BlockSpec((2 * PEER_HEADS, PEER_NKEYS, PEER_HALF), lambda i: (0, 0, 0))],
        out_specs=[ospec, ospec, ospec, ospec],
        out_shape=shp,
        scratch_shapes=[pltpu.VMEM((2 * PEER_HEADS, tt, PEER_HALF), BF16)],
        compiler_params=pltpu.CompilerParams(dimension_semantics=("arbitrary",),
                                             vmem_limit_bytes=VMEM_LIMIT),
        name="peer_sel",
    )(x1b, wq, sk)


PEER_ET = 2048
PEER_EQ = 512
PEER_LC = 256


def _peer_ffn_kernel(xt_ref, x1_ref, u_ref, vt_ref, s1_ref, e1_ref, th_ref, c_ref, g_ref, b_ref,
                     o_ref, acc_ref, h_ref, p_ref):
    e = pl.program_id(1)
    n_lc = xt_ref.shape[0]
    n_q = PEER_ET // PEER_EQ
    n_k = PEER_EQ // PEER_NKEYS
    assert n_lc == 2

    @pl.when(e == 0)
    def _():
        acc_ref[...] = jnp.zeros_like(acc_ref)

    def scores(q, lc):
        r0 = q * PEER_EQ if isinstance(q, int) else pl.multiple_of(q * PEER_EQ, PEER_EQ)
        h_ref[lc] = _dot(u_ref[pl.ds(r0, PEER_EQ), :], xt_ref[lc])

    def gated_act(q, lc):
        reps = PEER_NKEYS // PACKED_ROWS
        for k in range(n_k):
            rows = slice(k * PEER_NKEYS, (k + 1) * PEER_NKEYS)
            gate = jnp.zeros((PEER_NKEYS, PEER_LC), BF16)
            for h in range(PEER_HEADS):
                th = th_ref[lc, h, pl.ds(q * n_k + k, 1), :]
                cc = c_ref[lc, h, pl.ds(q * n_k + k, 1), :]
                th = jnp.tile(jnp.broadcast_to(th, (PACKED_ROWS, PEER_LC)).astype(jnp.int16), (reps, 1))
                cc = jnp.tile(jnp.broadcast_to(cc, (PACKED_ROWS, PEER_LC)).astype(BF16), (reps, 1))
                val = e1_ref[lc, h] * cc
                gate = gate + jnp.where(s1_ref[lc, h] >= th, val, jnp.zeros_like(val))
            hh = h_ref[lc, rows, :]
            act = hh + hh * lax.erf(hh * (2.0 ** -0.5))
            p_ref[lc, rows, :] = gate * act.astype(BF16)

    def accumulate(q, lc):
        acc_ref[lc] += _dot(vt_ref[q], p_ref[lc])

    scores(0, 0)
    scores(0, 1)
    gated_act(0, 0)

    def step(q, _):
        scores(q + 1, 0)
        gated_act(q, 1)
        accumulate(q, 0)
        scores(q + 1, 1)
        gated_act(q + 1, 0)
        accumulate(q, 1)
        return 0

    lax.fori_loop(0, n_q - 1, step, 0)
    gated_act(n_q - 1, 1)
    accumulate(n_q - 1, 0)
    accumulate(n_q - 1, 1)

    @pl.when(e == pl.num_programs(1) - 1)
    def _():
        for lc in range(n_lc):
            rows = slice(lc * PEER_LC, (lc + 1) * PEER_LC)
            y = ALPHA * x1_ref[rows, :] + acc_ref[lc].T
            o_ref[rows, :] = _layer_norm(y, g_ref[...], b_ref[...])


def _peer_ffn(xt, x1, u, vt, s1, e1, th, cc, g, b, tt):
    T, D = x1.shape
    E = u.shape[0]
    n_lc = tt // PEER_LC
    n_q = PEER_ET // PEER_EQ
    tok = pl.BlockSpec((n_lc, PEER_HEADS, PEER_NKEYS, PEER_LC), lambda t, e: (t, 0, 0, 0))
    sel = pl.BlockSpec((n_lc, PEER_HEADS, PEER_ET // PEER_NKEYS, PEER_LC), lambda t, e: (t, 0, e, 0))
    return pl.pallas_call(
        _peer_ffn_kernel,
        grid=(T // tt, E // PEER_ET),
        in_specs=[pl.BlockSpec((n_lc, D, PEER_LC), lambda t, e: (t, 0, 0)),
                  pl.BlockSpec((tt, D), lambda t, e: (t, 0)),
                  pl.BlockSpec((PEER_ET, D), lambda t, e: (e, 0)),
                  pl.BlockSpec((n_q, D, PEER_EQ), lambda t, e: (e, 0, 0)),
                  tok, tok, sel, sel,
                  pl.BlockSpec((1, D), lambda t, e: (0, 0)),
                  pl.BlockSpec((1, D), lambda t, e: (0, 0))],
        out_specs=pl.BlockSpec((tt, D), lambda t, e: (t, 0)),
        out_shape=jax.ShapeDtypeStruct((T, D), F32),
        scratch_shapes=[pltpu.VMEM((n_lc, D, PEER_LC), F32),
                        pltpu.VMEM((n_lc, PEER_EQ, PEER_LC), F32),
                        pltpu.VMEM((n_lc, PEER_EQ, PEER_LC), BF16)],
        compiler_params=pltpu.CompilerParams(dimension_semantics=("arbitrary", "arbitrary"),
                                             vmem_limit_bytes=VMEM_LIMIT),
        name="peer_ffn",
    )(xt, x1, u, vt, s1, e1, th, cc, g, b)


def _layer(x, w_in, w_gla_gate, b_gla_gate, gla_norm_g, sb_norm_g, w_out, ln1_g, ln1_b,
           peer_w_query, peer_sub_keys, peer_u, peer_v, ln2_g, ln2_b):
    B, S, D = x.shape
    T = B * S
    x2d = x.reshape(T, D)

    kw, gw, sw = GLA_HEADS * GLA_DK, GLA_HEADS * GLA_DV, SB_HEADS * SB_DH
    pts = [0, kw, 2 * kw, 2 * kw + gw, 2 * kw + 2 * gw, 2 * kw + 2 * gw + GLA_RANK]
    pts += [pts[-1] + sw, pts[-1] + 2 * sw, pts[-1] + 3 * sw]
    cols = [w_in[:, pts[n]:pts[n + 1]] for n in range(8)]
    cols[4] = jnp.pad(cols[4], ((0, 0), (0, LANES - GLA_RANK)))
    w_cat = jnp.concatenate(cols, axis=1).astype(BF16)
    wg = jnp.pad(w_gla_gate, ((0, LANES - GLA_RANK), (0, 0))).astype(BF16)

    qg, kg, vg, gg, alr, qs, ks, vs = _inproj(x2d, w_cat, min(TOKEN_TILE, T))

    ogla = _gla(qg, kg, vg, gg, alr, wg, b_gla_gate.reshape(1, -1), gla_norm_g.reshape(1, -1),
                B, S, min(GLA_SEQ_BLOCK, S))

    r = lax.broadcasted_iota(jnp.int32, (2 * SB_T, 2 * SB_T), 0) % SB_T
    c = lax.broadcasted_iota(jnp.int32, (2 * SB_T, 2 * SB_T), 1)
    tu = jnp.where((c >= SB_T) | (r > c), 1.0, 0.0).astype(BF16)
    osb = _sb(qs, ks, vs, tu, sb_norm_g.reshape(1, -1), B, S)

    x1, x1b = _outproj(ogla, osb, x2d, w_out.astype(BF16), ln1_g.reshape(1, D), ln1_b.reshape(1, D),
                       min(TOKEN_TILE, T))

    sk = peer_sub_keys.reshape(2 * PEER_HEADS, PEER_NKEYS, PEER_HALF).astype(BF16)
    s1, e1, th, cc = _peer_sel(x1b, peer_w_query.astype(BF16), sk, PEER_LC)

    xt = x1b.reshape(-1, PEER_LC, D).transpose(0, 2, 1)
    vt = peer_v.astype(BF16).reshape(-1, PEER_EQ, D).transpose(0, 2, 1)
    out = _peer_ffn(xt, x1, peer_u.astype(BF16), vt, s1, e1, th, cc,
                    ln2_g.reshape(1, D), ln2_b.reshape(1, D), 2 * PEER_LC)
    return out.reshape(B, S, D)


def kernel(x, w_in, w_gla_gate, b_gla_gate, gla_norm_g, sb_norm_g, w_out, ln1_g, ln1_b,
           peer_w_query, peer_sub_keys, peer_u, peer_v, ln2_g, ln2_b):
    for l in range(DEPTH):
        x = _layer(x, w_in[l], w_gla_gate[l], b_gla_gate[l], gla_norm_g[l], sb_norm_g[l], w_out[l],
                   ln1_g[l], ln1_b[l], peer_w_query[l], peer_sub_keys[l], peer_u[l], peer_v[l],
                   ln2_g[l], ln2_b[l])
    return x
```

```python
import functools

import jax
import jax.numpy as jnp
from jax import lax
from jax.experimental import pallas as pl
from jax.experimental.pallas import tpu as pltpu

F32 = jnp.float32
BF16 = jnp.bfloat16

LANES = 128
GLA_HEADS = 4
GLA_DK = 64
GLA_DV = 128
GLA_RANK = 16
GLA_GATE_TEMP = 16.0
GLA_CHUNK = 64
SB_HEADS = 8
SB_DH = 64
PEER_HEADS = 8
PEER_NKEYS = 128
PEER_HALF = 128
PEER_TOPK = 16
PEER_QONE = 16384.0
DEPTH = 1
ALPHA = (2.0 * DEPTH) ** 0.25
EPS = 1e-5
NEG_INF = float("-inf")
SB_DEAD = -104.0

VMEM_CAPACITY_V7X = 64 * 1024 * 1024
VMEM_LIMIT = VMEM_CAPACITY_V7X * 7 // 8
TOKEN_TILE = 512
GLA_SEQ_BLOCK = 512


def _dot(a, b):
    return jnp.dot(a, b, preferred_element_type=F32)


def _dot_nt(a, b):
    return lax.dot_general(a, b, (((1,), (1,)), ((), ())), preferred_element_type=F32)


def _dot_tn(a, b):
    return lax.dot_general(a, b, (((0,), (0,)), ((), ())), preferred_element_type=F32)


def _split_bf16(v):
    hi = v.astype(BF16)
    lo = (v - hi.astype(F32)).astype(BF16)
    return hi, lo


def _log_sigmoid(z):
    return jnp.minimum(z, 0.0) - jnp.log(1.0 + jnp.exp(-jnp.abs(z)))


IN_SPLITS = (("qg", 256, F32), ("kg", 256, F32), ("vg", 512, BF16), ("gg", 512, F32),
             ("alr", LANES, F32), ("qs", 512, BF16), ("ks", 512, BF16), ("vs", 512, BF16))


def _inproj_kernel(x_ref, w_ref, *out_refs):
    xb = x_ref[...].astype(BF16)
    off = 0
    for (name, width, _), o_ref in zip(IN_SPLITS, out_refs):
        r = _dot(xb, w_ref[:, off:off + width])
        if name in ("qg", "qs"):
            r = r * (GLA_DK ** -0.5)
        o_ref[...] = r.astype(o_ref.dtype)
        off += width


def _inproj(x2d, w_cat, tm):
    T, D = x2d.shape
    wtot = w_cat.shape[1]
    return pl.pallas_call(
        _inproj_kernel,
        grid=(T // tm,),
        in_specs=[pl.BlockSpec((tm, D), lambda i: (i, 0)),
                  pl.BlockSpec((D, wtot), lambda i: (0, 0))],
        out_specs=[pl.BlockSpec((tm, w), lambda i: (i, 0)) for _, w, _ in IN_SPLITS],
        out_shape=[jax.ShapeDtypeStruct((T, w), dt) for _, w, dt in IN_SPLITS],
        compiler_params=pltpu.CompilerParams(dimension_semantics=("arbitrary",),
                                             vmem_limit_bytes=VMEM_LIMIT),
        name="inproj",
    )(x2d, w_cat)


def _gla_kernel(q_ref, k_ref, v_ref, g_ref, a_ref, wg_ref, bg_ref, gn_ref, o_ref, st_ref, *, n_chunks):
    @pl.when(pl.program_id(0) == 0)
    def _():
        st_ref[...] = jnp.zeros_like(st_ref)

    C = GLA_CHUNK
    KW, VW = 2 * GLA_DK, 2 * GLA_DV
    lane = lax.broadcasted_iota(jnp.int32, (1, KW), 1)
    head_mask = [lane < GLA_DK, lane >= GLA_DK]
    r_i = lax.broadcasted_iota(jnp.int32, (C, C), 0)
    c_i = lax.broadcasted_iota(jnp.int32, (C, C), 1)
    causal = c_i <= r_i
    tri = jnp.where(causal, 1.0, 0.0).astype(BF16)

    n_batch = q_ref.shape[0]
    la_all = [_log_sigmoid(_dot(a_ref[bi].astype(BF16), wg_ref[...]) + bg_ref[...]) * (1.0 / GLA_GATE_TEMP)
              for bi in range(n_batch)]

    for c in range(n_chunks):
        rows = slice(c * C, (c + 1) * C)
        for bi in range(n_batch):
            la = la_all[bi][rows]
            hi, lo = _split_bf16(la)
            b_all = _dot(tri, hi) + _dot(tri, lo)
            for hp in range(GLA_HEADS // 2):
                kcols = slice(hp * KW, (hp + 1) * KW)
                vcols = slice(hp * VW, (hp + 1) * VW)
                b = b_all[:, kcols]
                b_last = b[C - 1:C, :]
                q = q_ref[bi, rows, kcols]
                k = k_ref[bi, rows, kcols]
                q_in = q * jnp.exp(b)
                k_in = (k * jnp.exp(-b)).astype(BF16)
                k_st = (k * jnp.exp(b_last - b)).astype(BF16)
                dec = jnp.exp(b_last)
                v = v_ref[bi, rows, vcols]
                st = st_ref[bi, hp]
                st_b = st.astype(BF16)
                outs = []
                for h in range(2):
                    qm = jnp.where(head_mask[h], q_in, 0.0).astype(BF16)
                    att = jnp.where(causal, _dot_nt(qm, k_in), 0.0)
                    vh = v[:, h * GLA_DV:(h + 1) * GLA_DV]
                    o_h = _dot(att.astype(BF16), vh) + _dot_nt(qm, st_b[h * GLA_DV:(h + 1) * GLA_DV, :])
                    ms = jnp.mean(o_h * o_h, axis=-1, keepdims=True)
                    outs.append(o_h * lax.rsqrt(ms + EPS))
                st_ref[bi, hp] = st * dec + _dot_tn(v, k_st)
                o = jnp.concatenate(outs, axis=1) * gn_ref[:, vcols]
                g = g_ref[bi, rows, vcols]
                o_ref[bi, rows, vcols] = (o * (g * jax.nn.sigmoid(g))).astype(o_ref.dtype)


def _gla(qg, kg, vg, gg, alr, wg, bg, gn, B, S, ls):
    kw, vw = GLA_HEADS * GLA_DK, GLA_HEADS * GLA_DV
    seq = lambda s: (0, s, 0)
    par = lambda s: (0, 0)
    per_batch = lambda t: t.reshape(B, S, t.shape[-1])
    out = pl.pallas_call(
        functools.partial(_gla_kernel, n_chunks=ls // GLA_CHUNK),
        grid=(S // ls,),
        in_specs=[pl.BlockSpec((B, ls, kw), seq),
                  pl.BlockSpec((B, ls, kw), seq),
                  pl.BlockSpec((B, ls, vw), seq),
                  pl.BlockSpec((B, ls, vw), seq),
                  pl.BlockSpec((B, ls, LANES), seq),
                  pl.BlockSpec((LANES, kw), par),
                  pl.BlockSpec((1, kw), par),
                  pl.BlockSpec((1, vw), par)],
        out_specs=pl.BlockSpec((B, ls, vw), seq),
        out_shape=jax.ShapeDtypeStruct((B, S, vw), BF16),
        scratch_shapes=[pltpu.VMEM((B, GLA_HEADS // 2, 2 * GLA_DV, 2 * GLA_DK), F32)],
        compiler_params=pltpu.CompilerParams(dimension_semantics=("arbitrary",),
                                             vmem_limit_bytes=VMEM_LIMIT),
        name="gla",
    )(per_batch(qg), per_batch(kg), per_batch(vg), per_batch(gg), per_batch(alr), wg, bg, gn)
    return out.reshape(B * S, vw)


SB_T = 128
SB_QB = 4
SB_GW = 2 * LANES
SB_GROUPS = SB_HEADS * SB_DH // SB_GW
SB_TILES = 4 * SB_GROUPS * SB_QB


def _sb_kernel(q_ref, k_ref, v_ref, tu_ref, gn_ref, o_ref,
               acc_ref, car_ref, kb_ref, vb_ref, zl_ref, l_ref, cs_ref):
    i = pl.program_id(1)
    T = SB_T
    lane = lax.broadcasted_iota(jnp.int32, (1, SB_GW), 1)
    is_h0 = (lane % LANES) < SB_DH
    qg = []
    for a in range(SB_QB):
        qa = []
        for g in range(SB_GROUPS):
            q = q_ref[a * T:(a + 1) * T, g * SB_GW:(g + 1) * SB_GW]
            zero = jnp.zeros_like(q)
            qa.append(jnp.concatenate([jnp.where(is_h0, q, zero), jnp.where(is_h0, zero, q)], axis=0))
        qg.append(qa)
    row = lax.broadcasted_iota(jnp.int32, (T, T), 0)
    col = lax.broadcasted_iota(jnp.int32, (T, T), 1)
    acc_ref[...] = jnp.zeros_like(acc_ref)
    car_ref[...] = jnp.zeros_like(car_ref)
    kb_ref[...] = jnp.zeros_like(kb_ref)
    vb_ref[...] = jnp.zeros_like(vb_ref)

    def sweep(m, diagonal):
        valid = col < row
        for a in range(SB_QB):
            j = SB_QB * i + a - m
            ks = pl.multiple_of(jnp.maximum(j, 0) * T, T)
            for g in range(SB_GROUPS):
                s = a * SB_GROUPS + g
                for p in range(2):
                    src = slice(g * SB_GW + p * LANES, g * SB_GW + (p + 1) * LANES)
                    kb_ref[s, p * T:(p + 1) * T, p * LANES:(p + 1) * LANES] = k_ref[pl.ds(ks, T), src]
                    vb_ref[s, p * T:(p + 1) * T, p * LANES:(p + 1) * LANES] = v_ref[pl.ds(ks, T), src]
                z = _dot_nt(qg[a][g], kb_ref[s])
                for h in range(2):
                    for p in range(2):
                        n = 4 * s + 2 * h + p
                        zq = z[h * T:(h + 1) * T, p * T:(p + 1) * T]
                        lnb = -(jnp.maximum(zq, 0.0) + jnp.log(1.0 + jnp.exp(-jnp.abs(zq))))
                        if diagonal:
                            lnb = jnp.where(valid, lnb, 0.0)
                        hi, lo = _split_bf16(lnb)
                        l_ref[n * T:(n + 1) * T, :T] = hi
                        l_ref[n * T:(n + 1) * T, T:] = lo
                        zl_ref[n] = zq + lnb
        cs_ref[...] = _dot(l_ref[...], tu_ref[...])
        top = None
        for a in range(SB_QB):
            j = SB_QB * i + a - m
            spent = jnp.where(j < 0, -1e30, 0.0).astype(F32)
            for g in range(SB_GROUPS):
                s = a * SB_GROUPS + g
                w_rows = []
                for h in range(2):
                    w_cols = []
                    for p in range(2):
                        n = 4 * s + 2 * h + p
                        car = car_ref[n] if diagonal else car_ref[n] + spent
                        w = jnp.exp(zl_ref[n] + cs_ref[n * T:(n + 1) * T, :T] + car)
                        if diagonal:
                            w = jnp.where(valid, w, 0.0)
                        w_cols.append(w.astype(BF16))
                        car = car + cs_ref[n * T:(n + 1) * T, T:]
                        car_ref[n] = car
                        top = car if top is None else jnp.maximum(top, car)
                    w_rows.append(jnp.concatenate(w_cols, axis=1))
                acc_ref[s] += _dot(jnp.concatenate(w_rows, axis=0), vb_ref[s])
        return jnp.max(top)

    def cond(carry):
        m, alive = carry
        return jnp.logical_and(m <= SB_QB * i + SB_QB - 1, alive > SB_DEAD)

    def body(carry):
        m, _ = carry
        return m + 1, sweep(m, diagonal=False)

    lax.while_loop(cond, body, (jnp.int32(1), sweep(0, diagonal=True)))

    head_of_lane = lane // SB_DH
    for a in range(SB_QB):
        for g in range(SB_GROUPS):
            acc = acc_ref[a * SB_GROUPS + g]
            o = jnp.where(is_h0, acc[:T], acc[T:])
            sq = o * o
            ms = jnp.zeros_like(o)
            for hd in range(SB_GW // SB_DH):
                mine = head_of_lane == hd
                ms = jnp.where(mine, jnp.sum(jnp.where(mine, sq, 0.0), axis=-1, keepdims=True), ms)
            cols = slice(g * SB_GW, (g + 1) * SB_GW)
            o_ref[a * T:(a + 1) * T, cols] = (o * lax.rsqrt(ms * (1.0 / SB_DH) + EPS) * gn_ref[:, cols]).astype(o_ref.dtype)


def _sb(qs, ks, vs, tu, gn, B, S):
    rows = SB_QB * SB_T
    n_q = S // rows
    width = SB_HEADS * SB_DH
    n_slab = SB_QB * SB_GROUPS
    return pl.pallas_call(
        _sb_kernel,
        grid=(B, n_q),
        in_specs=[pl.BlockSpec((rows, width), lambda b, i: (b * n_q + i, 0)),
                  pl.BlockSpec((S, width), lambda b, i: (b, 0)),
                  pl.BlockSpec((S, width), lambda b, i: (b, 0)),
                  pl.BlockSpec((2 * SB_T, 2 * SB_T), lambda b, i: (0, 0)),
                  pl.BlockSpec((1, width), lambda b, i: (0, 0))],
        out_specs=pl.BlockSpec((rows, width), lambda b, i: (b * n_q + i, 0)),
        out_shape=jax.ShapeDtypeStruct((B * S, width), BF16),
        scratch_shapes=[pltpu.VMEM((n_slab, 2 * SB_T, SB_GW), F32),
                        pltpu.VMEM((SB_TILES, SB_T, SB_T), F32),
                        pltpu.VMEM((n_slab, 2 * SB_T, SB_GW), BF16),
                        pltpu.VMEM((n_slab, 2 * SB_T, SB_GW), BF16),
                        pltpu.VMEM((SB_TILES, SB_T, SB_T), F32),
                        pltpu.VMEM((SB_TILES * SB_T, 2 * SB_T), BF16),
                        pltpu.VMEM((SB_TILES * SB_T, 2 * SB_T), F32)],
        compiler_params=pltpu.CompilerParams(dimension_semantics=("arbitrary", "arbitrary"),
                                             vmem_limit_bytes=VMEM_LIMIT),
        name="sb",
    )(qs, ks, vs, tu, gn)


def _layer_norm(y, g, b):
    mu = jnp.mean(y, axis=-1, keepdims=True)
    d = y - mu
    var = jnp.mean(d * d, axis=-1, keepdims=True)
    return d * lax.rsqrt(var + EPS) * g + b


def _outproj_kernel(og_ref, os_ref, x_ref, wo_ref, g_ref, b_ref, x1_ref, x1b_ref):
    half = og_ref.shape[1]
    mix = _dot(og_ref[...], wo_ref[:half, :]) + _dot(os_ref[...], wo_ref[half:, :])
    x1 = _layer_norm(ALPHA * x_ref[...] + mix, g_ref[...], b_ref[...])
    x1_ref[...] = x1
    x1b_ref[...] = x1.astype(BF16)


def _outproj(ogla, osb, x2d, wo, g, b, tm):
    T, D = x2d.shape
    half = ogla.shape[1]
    return pl.pallas_call(
        _outproj_kernel,
        grid=(T // tm,),
        in_specs=[pl.BlockSpec((tm, half), lambda i: (i, 0)),
                  pl.BlockSpec((tm, half), lambda i: (i, 0)),
                  pl.BlockSpec((tm, D), lambda i: (i, 0)),
                  pl.BlockSpec((2 * half, D), lambda i: (0, 0)),
                  pl.BlockSpec((1, D), lambda i: (0, 0)),
                  pl.BlockSpec((1, D), lambda i: (0, 0))],
        out_specs=[pl.BlockSpec((tm, D), lambda i: (i, 0)),
                   pl.BlockSpec((tm, D), lambda i: (i, 0))],
        out_shape=[jax.ShapeDtypeStruct((T, D), F32), jax.ShapeDtypeStruct((T, D), BF16)],
        compiler_params=pltpu.CompilerParams(dimension_semantics=("arbitrary",),
                                             vmem_limit_bytes=VMEM_LIMIT),
        name="outproj",
    )(ogla, osb, x2d, wo, g, b)


PEER_SEL_HEADS_PER_BODY = 8
SUBLANES = 8
PACKED_ROWS = 16


def _batcher_network(n):
    pairs, p = [], 1
    while p < n:
        k = p
        while k >= 1:
            for j in range(k % p, n - k, 2 * k):
                for i in range(min(k, n - j - k)):
                    if (i + j) // (2 * p) == (i + j + k) // (2 * p):
                        pairs.append((i + j, i + j + k))
            k //= 2
        p *= 2
    return pairs


def _merge_top(stacks, n):
    stacks = [list(st) for st in stacks]
    tops = []
    for r in range(n):
        head = stacks[0][0]
        for st in stacks[1:]:
            head = jnp.maximum(head, st[0])
        m = jnp.max(head, axis=0, keepdims=True)
        tops.append(m)
        need = n - r - 1
        for st in stacks:
            if need == 0:
                break
            hit = st[0] == m
            depth = min(need, len(st))
            for d in range(depth):
                below = st[d + 1] if d + 1 < len(st) else NEG_INF
                st[d] = jnp.where(hit, below, st[d])
            del st[depth:]
    return tops


def _top_rows(s, n):
    slabs = [s[r:r + SUBLANES] for r in range(0, s.shape[0], SUBLANES)]
    for i, j in _batcher_network(len(slabs)):
        slabs[i], slabs[j] = jnp.maximum(slabs[i], slabs[j]), jnp.minimum(slabs[i], slabs[j])
    return _merge_top([slabs], n)


def _peer_sel_kernel(x_ref, wq_ref, sk_ref, s1_ref, e1_ref, th_ref, c_ref, q_scr):
    tt = x_ref.shape[0]
    qry = _dot(x_ref[...], wq_ref[...]).astype(BF16)
    for hp in range(2 * PEER_HEADS):
        q_scr[hp] = qry[:, hp * PEER_HALF:(hp + 1) * PEER_HALF]
    sub = lax.broadcasted_iota(jnp.int32, (SUBLANES, tt), 0)

    def head(h):
        s0 = _dot_nt(sk_ref[2 * h], q_scr[2 * h])
        s1 = _dot_nt(sk_ref[2 * h + 1], q_scr[2 * h + 1])
        n = PEER_TOPK + 1
        a = _top_rows(s0, n)
        b = _top_rows(s1, n)
        a_lo = jnp.concatenate(a[:SUBLANES], axis=0)
        a_hi = jnp.concatenate(a[SUBLANES:2 * SUBLANES], axis=0)
        xs = [jnp.where(sub < n // l, a_lo + b[l - 1], NEG_INF) for l in range(1, n + 1)]
        zs = [a_hi + b[0], jnp.where(sub < 1, a[n - 1] + b[0], NEG_INF)]
        best = _merge_top([xs, zs], n)
        z = jnp.zeros_like(best[0])
        for r in range(PEER_TOPK):
            z = z + jnp.exp(best[r] - best[0])
        tau = 0.5 * (best[PEER_TOPK - 1] + best[PEER_TOPK])
        d1 = s1 - b[0]
        d0 = s0 - a[0]
        tau_rel = tau - best[0]
        scale = PEER_QONE / jnp.maximum(-tau_rel, 1e-30)
        s1_ref[0, h] = jnp.maximum(d1 * scale, -2.0 * PEER_QONE).astype(jnp.int32).astype(jnp.int16)
        th_ref[0, h] = jnp.minimum((tau_rel - d0) * scale, 2.0 * PEER_QONE - 1.0).astype(jnp.int32)
        e1_ref[0, h] = jnp.exp(d1).astype(BF16)
        c_ref[0, h] = jnp.exp(d0) * (0.5 / z)

    def head_group(i, _):
        for n in range(PEER_SEL_HEADS_PER_BODY):
            head(PEER_SEL_HEADS_PER_BODY * i + n)
        return 0

    lax.fori_loop(0, PEER_HEADS // PEER_SEL_HEADS_PER_BODY, head_group, 0)


def _peer_sel(x1b, wq, sk, tt):
    T, D = x1b.shape
    qd = wq.shape[1]
    shp = [jax.ShapeDtypeStruct((T // tt, PEER_HEADS, PEER_NKEYS, tt), dt)
           for dt in (jnp.int16, BF16, jnp.int32, F32)]
    ospec = pl.BlockSpec((1, PEER_HEADS, PEER_NKEYS, tt), lambda i: (i, 0, 0, 0))
    return pl.pallas_call(
        _peer_sel_kernel,
        grid=(T // tt,),
        in_specs=[pl.BlockSpec((tt, D), lambda i: (i, 0)),
                  pl.BlockSpec((D, qd), lambda i: (0, 0)),
                  pl.BlockSpec((2 * PEER_HEADS, PEER_NKEYS, PEER_HALF), lambda i: (0, 0, 0))],
        out_specs=[ospec, ospec, ospec, ospec],
        out_shape=shp,
        scratch_shapes=[pltpu.VMEM((2 * PEER_HEADS, tt, PEER_HALF), BF16)],
        compiler_params=pltpu.CompilerParams(dimension_semantics=("arbitrary",),
                                             vmem_limit_bytes=VMEM_LIMIT),
        name="peer_sel",
    )(x1b, wq, sk)


PEER_ET = 2048
PEER_EQ = 512
PEER_LC = 256


def _peer_ffn_kernel(xt_ref, x1_ref, u_ref, vt_ref, s1_ref, e1_ref, th_ref, c_ref, g_ref, b_ref,
                     o_ref, acc_ref, h_ref, p_ref):
    e = pl.program_id(1)
    n_lc = xt_ref.shape[0]
    n_q = PEER_ET // PEER_EQ
    n_k = PEER_EQ // PEER_NKEYS
    assert n_lc == 2

    @pl.when(e == 0)
    def _():
        acc_ref[...] = jnp.zeros_like(acc_ref)

    def scores(q, lc):
        r0 = q * PEER_EQ if isinstance(q, int) else pl.multiple_of(q * PEER_EQ, PEER_EQ)
        h_ref[lc] = _dot(u_ref[pl.ds(r0, PEER_EQ), :], xt_ref[lc])

    def gated_act(q, lc):
        reps = PEER_NKEYS // PACKED_ROWS
        for k in range(n_k):
            rows = slice(k * PEER_NKEYS, (k + 1) * PEER_NKEYS)
            gate = jnp.zeros((PEER_NKEYS, PEER_LC), BF16)
            for h in range(PEER_HEADS):
                th = th_ref[lc, h, pl.ds(q * n_k + k, 1), :]
                cc = c_ref[lc, h, pl.ds(q * n_k + k, 1), :]
                th = jnp.tile(jnp.broadcast_to(th, (PACKED_ROWS, PEER_LC)).astype(jnp.int16), (reps, 1))
                cc = jnp.tile(jnp.broadcast_to(cc, (PACKED_ROWS, PEER_LC)).astype(BF16), (reps, 1))
                val = e1_ref[lc, h] * cc
                gate = gate + jnp.where(s1_ref[lc, h] >= th, val, jnp.zeros_like(val))
            hh = h_ref[lc, rows, :]
            act = hh + hh * lax.erf(hh * (2.0 ** -0.5))
            p_ref[lc, rows, :] = gate * act.astype(BF16)

    def accumulate(q, lc):
        acc_ref[lc] += _dot(vt_ref[q], p_ref[lc])

    scores(0, 0)
    scores(0, 1)
    gated_act(0, 0)

    def step(q, _):
        scores(q + 1, 0)
        gated_act(q, 1)
        accumulate(q, 0)
        scores(q + 1, 1)
        gated_act(q + 1, 0)
        accumulate(q, 1)
        return 0

    lax.fori_loop(0, n_q - 1, step, 0)
    gated_act(n_q - 1, 1)
    accumulate(n_q - 1, 0)
    accumulate(n_q - 1, 1)

    @pl.when(e == pl.num_programs(1) - 1)
    def _():
        for lc in range(n_lc):
            rows = slice(lc * PEER_LC, (lc + 1) * PEER_LC)
            y = ALPHA * x1_ref[rows, :] + acc_ref[lc].T
            o_ref[rows, :] = _layer_norm(y, g_ref[...], b_ref[...])


def _peer_ffn(xt, x1, u, vt, s1, e1, th, cc, g, b, tt):
    T, D = x1.shape
    E = u.shape[0]
    n_lc = tt // PEER_LC
    n_q = PEER_ET // PEER_EQ
    tok = pl.BlockSpec((n_lc, PEER_HEADS, PEER_NKEYS, PEER_LC), lambda t, e: (t, 0, 0, 0))
    sel = pl.BlockSpec((n_lc, PEER_HEADS, PEER_ET // PEER_NKEYS, PEER_LC), lambda t, e: (t, 0, e, 0))
    return pl.pallas_call(
        _peer_ffn_kernel,
        grid=(T // tt, E // PEER_ET),
        in_specs=[pl.BlockSpec((n_lc, D, PEER_LC), lambda t, e: (t, 0, 0)),
                  pl.BlockSpec((tt, D), lambda t, e: (t, 0)),
                  pl.BlockSpec((PEER_ET, D), lambda t, e: (e, 0)),
                  pl.BlockSpec((n_q, D, PEER_EQ), lambda t, e: (e, 0, 0)),
                  tok, tok, sel, sel,
                  pl.BlockSpec((1, D), lambda t, e: (0, 0)),
                  pl.BlockSpec((1, D), lambda t, e: (0, 0))],
        out_specs=pl.BlockSpec((tt, D), lambda t, e: (t, 0)),
        out_shape=jax.ShapeDtypeStruct((T, D), F32),
        scratch_shapes=[pltpu.VMEM((n_lc, D, PEER_LC), F32),
                        pltpu.VMEM((n_lc, PEER_EQ, PEER_LC), F32),
                        pltpu.VMEM((n_lc, PEER_EQ, PEER_LC), BF16)],
        compiler_params=pltpu.CompilerParams(dimension_semantics=("arbitrary", "arbitrary"),
                                             vmem_limit_bytes=VMEM_LIMIT),
        name="peer_ffn",
    )(xt, x1, u, vt, s1, e1, th, cc, g, b)


def _layer(x, w_in, w_gla_gate, b_gla_gate, gla_norm_g, sb_norm_g, w_out, ln1_g, ln1_b,
           peer_w_query, peer_sub_keys, peer_u, peer_v, ln2_g, ln2_b):
    B, S, D = x.shape
    T = B * S
    x2d = x.reshape(T, D)

    kw, gw, sw = GLA_HEADS * GLA_DK, GLA_HEADS * GLA_DV, SB_HEADS * SB_DH
    pts = [0, kw, 2 * kw, 2 * kw + gw, 2 * kw + 2 * gw, 2 * kw + 2 * gw + GLA_RANK]
    pts += [pts[-1] + sw, pts[-1] + 2 * sw, pts[-1] + 3 * sw]
    cols = [w_in[:, pts[n]:pts[n + 1]] for n in range(8)]
    cols[4] = jnp.pad(cols[4], ((0, 0), (0, LANES - GLA_RANK)))
    w_cat = jnp.concatenate(cols, axis=1).astype(BF16)
    wg = jnp.pad(w_gla_gate, ((0, LANES - GLA_RANK), (0, 0))).astype(BF16)

    qg, kg, vg, gg, alr, qs, ks, vs = _inproj(x2d, w_cat, min(TOKEN_TILE, T))

    ogla = _gla(qg, kg, vg, gg, alr, wg, b_gla_gate.reshape(1, -1), gla_norm_g.reshape(1, -1),
                B, S, min(GLA_SEQ_BLOCK, S))

    r = lax.broadcasted_iota(jnp.int32, (2 * SB_T, 2 * SB_T), 0) % SB_T
    c = lax.broadcasted_iota(jnp.int32, (2 * SB_T, 2 * SB_T), 1)
    tu = jnp.where((c >= SB_T) | (r > c), 1.0, 0.0).astype(BF16)
    osb = _sb(qs, ks, vs, tu, sb_norm_g.reshape(1, -1), B, S)

    x1, x1b = _outproj(ogla, osb, x2d, w_out.astype(BF16), ln1_g.reshape(1, D), ln1_b.reshape(1, D),
                       min(TOKEN_TILE, T))

    sk = peer_sub_keys.reshape(2 * PEER_HEADS, PEER_NKEYS, PEER_HALF).astype(BF16)
    s1, e1, th, cc = _peer_sel(x1b, peer_w_query.astype(BF16), sk, PEER_LC)

    xt = x1b.reshape(-1, PEER_LC, D).transpose(0, 2, 1)
    vt = peer_v.astype(BF16).reshape(-1, PEER_EQ, D).transpose(0, 2, 1)
    out = _peer_ffn(xt, x1, peer_u.astype(BF16), vt, s1, e1, th, cc,
                    ln2_g.reshape(1, D), ln2_b.reshape(1, D), 2 * PEER_LC)
    return out.reshape(B, S, D)


def kernel(x, w_in, w_gla_gate, b_gla_gate, gla_norm_g, sb_norm_g, w_out, ln1_g, ln1_b,
           peer_w_query, peer_sub_keys, peer_u, peer_v, ln2_g, ln2_b):
    for l in range(DEPTH):
        x = _layer(x, w_in[l], w_gla_gate[l], b_gla_gate[l], gla_norm_g[l], sb_norm_g[l], w_out[l],
                   ln1_g[l], ln1_b[l], peer_w_query[l], peer_sub_keys[l], peer_u[l], peer_v[l],
                   ln2_g[l], ln2_b[l])
    return x
```

```python
import functools

import jax
import jax.numpy as jnp
from jax import lax
from jax.experimental import pallas as pl
from jax.experimental.pallas import tpu as pltpu

F32 = jnp.float32
BF16 = jnp.bfloat16

LANES = 128
GLA_HEADS = 4
GLA_DK = 64
GLA_DV = 128
GLA_RANK = 16
GLA_GATE_TEMP = 16.0
GLA_CHUNK = 64
SB_HEADS = 8
SB_DH = 64
PEER_HEADS = 8
PEER_NKEYS = 128
PEER_HALF = 128
PEER_TOPK = 16
PEER_QONE = 16384.0
DEPTH = 1
ALPHA = (2.0 * DEPTH) ** 0.25
EPS = 1e-5
NEG_INF = float("-inf")
SB_DEAD = -104.0

VMEM_CAPACITY_V7X = 64 * 1024 * 1024
VMEM_LIMIT = VMEM_CAPACITY_V7X * 7 // 8
TOKEN_TILE = 512
GLA_SEQ_BLOCK = 512


def _dot(a, b):
    return jnp.dot(a, b, preferred_element_type=F32)


def _dot_nt(a, b):
    return lax.dot_general(a, b, (((1,), (1,)), ((), ())), preferred_element_type=F32)


def _dot_tn(a, b):
    return lax.dot_general(a, b, (((0,), (0,)), ((), ())), preferred_element_type=F32)


def _split_bf16(v):
    hi = v.astype(BF16)
    lo = (v - hi.astype(F32)).astype(BF16)
    return hi, lo


def _log_sigmoid(z):
    return jnp.minimum(z, 0.0) - jnp.log(1.0 + jnp.exp(-jnp.abs(z)))


IN_SPLITS = (("qg", 256, F32), ("kg", 256, F32), ("vg", 512, BF16), ("gg", 512, F32),
             ("alr", LANES, F32), ("qs", 512, BF16), ("ks", 512, BF16), ("vs", 512, BF16))


def _inproj_kernel(x_ref, w_ref, *out_refs):
    xb = x_ref[...].astype(BF16)
    off = 0
    for (name, width, _), o_ref in zip(IN_SPLITS, out_refs):
        r = _dot(xb, w_ref[:, off:off + width])
        if name in ("qg", "qs"):
            r = r * (GLA_DK ** -0.5)
        o_ref[...] = r.astype(o_ref.dtype)
        off += width


def _inproj(x2d, w_cat, tm):
    T, D = x2d.shape
    wtot = w_cat.shape[1]
    return pl.pallas_call(
        _inproj_kernel,
        grid=(T // tm,),
        in_specs=[pl.BlockSpec((tm, D), lambda i: (i, 0)),
                  pl.BlockSpec((D, wtot), lambda i: (0, 0))],
        out_specs=[pl.BlockSpec((tm, w), lambda i: (i, 0)) for _, w, _ in IN_SPLITS],
        out_shape=[jax.ShapeDtypeStruct((T, w), dt) for _, w, dt in IN_SPLITS],
        compiler_params=pltpu.CompilerParams(dimension_semantics=("arbitrary",),
                                             vmem_limit_bytes=VMEM_LIMIT),
        name="inproj",
    )(x2d, w_cat)


def _gla_kernel(q_ref, k_ref, v_ref, g_ref, a_ref, wg_ref, bg_ref, gn_ref, o_ref, st_ref, *, n_chunks):
    @pl.when(pl.program_id(0) == 0)
    def _():
        st_ref[...] = jnp.zeros_like(st_ref)

    C = GLA_CHUNK
    KW, VW = 2 * GLA_DK, 2 * GLA_DV
    lane = lax.broadcasted_iota(jnp.int32, (1, KW), 1)
    head_mask = [lane < GLA_DK, lane >= GLA_DK]
    r_i = lax.broadcasted_iota(jnp.int32, (C, C), 0)
    c_i = lax.broadcasted_iota(jnp.int32, (C, C), 1)
    causal = c_i <= r_i
    tri = jnp.where(causal, 1.0, 0.0).astype(BF16)

    n_batch = q_ref.shape[0]
    la_all = [_log_sigmoid(_dot(a_ref[bi].astype(BF16), wg_ref[...]) + bg_ref[...]) * (1.0 / GLA_GATE_TEMP)
              for bi in range(n_batch)]

    for c in range(n_chunks):
        rows = slice(c * C, (c + 1) * C)
        for bi in range(n_batch):
            la = la_all[bi][rows]
            hi, lo = _split_bf16(la)
            b_all = _dot(tri, hi) + _dot(tri, lo)
            for hp in range(GLA_HEADS // 2):
                kcols = slice(hp * KW, (hp + 1) * KW)
                vcols = slice(hp * VW, (hp + 1) * VW)
                b = b_all[:, kcols]
                b_last = b[C - 1:C, :]
                q = q_ref[bi, rows, kcols]
                k = k_ref[bi, rows, kcols]
                q_in = q * jnp.exp(b)
                k_in = (k * jnp.exp(-b)).astype(BF16)
                k_st = (k * jnp.exp(b_last - b)).astype(BF16)
                dec = jnp.exp(b_last)
                v = v_ref[bi, rows, vcols]
                st = st_ref[bi, hp]
                st_b = st.astype(BF16)
                outs = []
                for h in range(2):
                    qm = jnp.where(head_mask[h], q_in, 0.0).astype(BF16)
                    att = jnp.where(causal, _dot_nt(qm, k_in), 0.0)
                    vh = v[:, h * GLA_DV:(h + 1) * GLA_DV]
                    o_h = _dot(att.astype(BF16), vh) + _dot_nt(qm, st_b[h * GLA_DV:(h + 1) * GLA_DV, :])
                    ms = jnp.mean(o_h * o_h, axis=-1, keepdims=True)
                    outs.append(o_h * lax.rsqrt(ms + EPS))
                st_ref[bi, hp] = st * dec + _dot_tn(v, k_st)
                o = jnp.concatenate(outs, axis=1) * gn_ref[:, vcols]
                g = g_ref[bi, rows, vcols]
                o_ref[bi, rows, vcols] = (o * (g * jax.nn.sigmoid(g))).astype(o_ref.dtype)


def _gla(qg, kg, vg, gg, alr, wg, bg, gn, B, S, ls):
    kw, vw = GLA_HEADS * GLA_DK, GLA_HEADS * GLA_DV
    seq = lambda s: (0, s, 0)
    par = lambda s: (0, 0)
    per_batch = lambda t: t.reshape(B, S, t.shape[-1])
    out = pl.pallas_call(
        functools.partial(_gla_kernel, n_chunks=ls // GLA_CHUNK),
        grid=(S // ls,),
        in_specs=[pl.BlockSpec((B, ls, kw), seq),
                  pl.BlockSpec((B, ls, kw), seq),
                  pl.BlockSpec((B, ls, vw), seq),
                  pl.BlockSpec((B, ls, vw), seq),
                  pl.BlockSpec((B, ls, LANES), seq),
                  pl.BlockSpec((LANES, kw), par),
                  pl.BlockSpec((1, kw), par),
                  pl.BlockSpec((1, vw), par)],
        out_specs=pl.BlockSpec((B, ls, vw), seq),
        out_shape=jax.ShapeDtypeStruct((B, S, vw), BF16),
        scratch_shapes=[pltpu.VMEM((B, GLA_HEADS // 2, 2 * GLA_DV, 2 * GLA_DK), F32)],
        compiler_params=pltpu.CompilerParams(dimension_semantics=("arbitrary",),
                                             vmem_limit_bytes=VMEM_LIMIT),
        name="gla",
    )(per_batch(qg), per_batch(kg), per_batch(vg), per_batch(gg), per_batch(alr), wg, bg, gn)
    return out.reshape(B * S, vw)


SB_T = 128
SB_QB = 4
SB_GW = 2 * LANES
SB_GROUPS = SB_HEADS * SB_DH // SB_GW
SB_TILES = 4 * SB_GROUPS * SB_QB


def _sb_kernel(q_ref, k_ref, v_ref, tu_ref, gn_ref, o_ref,
               acc_ref, car_ref, kb_ref, vb_ref, zl_ref, l_ref, cs_ref):
    i = pl.program_id(1)
    T = SB_T
    lane = lax.broadcasted_iota(jnp.int32, (1, SB_GW), 1)
    is_h0 = (lane % LANES) < SB_DH
    qg = []
    for a in range(SB_QB):
        qa = []
        for g in range(SB_GROUPS):
            q = q_ref[a * T:(a + 1) * T, g * SB_GW:(g + 1) * SB_GW]
            zero = jnp.zeros_like(q)
            qa.append(jnp.concatenate([jnp.where(is_h0, q, zero), jnp.where(is_h0, zero, q)], axis=0))
        qg.append(qa)
    row = lax.broadcasted_iota(jnp.int32, (T, T), 0)
    col = lax.broadcasted_iota(jnp.int32, (T, T), 1)
    acc_ref[...] = jnp.zeros_like(acc_ref)
    car_ref[...] = jnp.zeros_like(car_ref)
    kb_ref[...] = jnp.zeros_like(kb_ref)
    vb_ref[...] = jnp.zeros_like(vb_ref)

    def sweep(m, diagonal):
        valid = col < row
        for a in range(SB_QB):
            j = SB_QB * i + a - m
            ks = pl.multiple_of(jnp.maximum(j, 0) * T, T)
            for g in range(SB_GROUPS):
                s = a * SB_GROUPS + g
                for p in range(2):
                    src = slice(g * SB_GW + p * LANES, g * SB_GW + (p + 1) * LANES)
                    kb_ref[s, p * T:(p + 1) * T, p * LANES:(p + 1) * LANES] = k_ref[pl.ds(ks, T), src]
                    vb_ref[s, p * T:(p + 1) * T, p * LANES:(p + 1) * LANES] = v_ref[pl.ds(ks, T), src]
                z = _dot_nt(qg[a][g], kb_ref[s])
                for h in range(2):
                    for p in range(2):
                        n = 4 * s + 2 * h + p
                        zq = z[h * T:(h + 1) * T, p * T:(p + 1) * T]
                        lnb = -(jnp.maximum(zq, 0.0) + jnp.log(1.0 + jnp.exp(-jnp.abs(zq))))
                        if diagonal:
                            lnb = jnp.where(valid, lnb, 0.0)
                        hi, lo = _split_bf16(lnb)
                        l_ref[n * T:(n + 1) * T, :T] = hi
                        l_ref[n * T:(n + 1) * T, T:] = lo
                        zl_ref[n] = zq + lnb
        cs_ref[...] = _dot(l_ref[...], tu_ref[...])
        top = None
        for a in range(SB_QB):
            j = SB_QB * i + a - m
            spent = jnp.where(j < 0, -1e30, 0.0).astype(F32)
            for g in range(SB_GROUPS):
                s = a * SB_GROUPS + g
                w_rows = []
                for h in range(2):
                    w_cols = []
                    for p in range(2):
                        n = 4 * s + 2 * h + p
                        car = car_ref[n] if diagonal else car_ref[n] + spent
                        w = jnp.exp(zl_ref[n] + cs_ref[n * T:(n + 1) * T, :T] + car)
                        if diagonal:
                            w = jnp.where(valid, w, 0.0)
                        w_cols.append(w.astype(BF16))
                        car = car + cs_ref[n * T:(n + 1) * T, T:]
                        car_ref[n] = car
                        top = car if top is None else jnp.maximum(top, car)
                    w_rows.append(jnp.concatenate(w_cols, axis=1))
                acc_ref[s] += _dot(jnp.concatenate(w_rows, axis=0), vb_ref[s])
        return jnp.max(top)

    def cond(carry):
        m, alive = carry
        return jnp.logical_and(m <= SB_QB * i + SB_QB - 1, alive > SB_DEAD)

    def body(carry):
        m, _ = carry
        return m + 1, sweep(m, diagonal=False)

    lax.while_loop(cond, body, (jnp.int32(1), sweep(0, diagonal=True)))

    head_of_lane = lane // SB_DH
    for a in range(SB_QB):
        for g in range(SB_GROUPS):
            acc = acc_ref[a * SB_GROUPS + g]
            o = jnp.where(is_h0, acc[:T], acc[T:])
            sq = o * o
            ms = jnp.zeros_like(o)
            for hd in range(SB_GW // SB_DH):
                mine = head_of_lane == hd
                ms = jnp.where(mine, jnp.sum(jnp.where(mine, sq, 0.0), axis=-1, keepdims=True), ms)
            cols = slice(g * SB_GW, (g + 1) * SB_GW)
            o_ref[a * T:(a + 1) * T, cols] = (o * lax.rsqrt(ms * (1.0 / SB_DH) + EPS) * gn_ref[:, cols]).astype(o_ref.dtype)


def _sb(qs, ks, vs, tu, gn, B, S):
    rows = SB_QB * SB_T
    n_q = S // rows
    width = SB_HEADS * SB_DH
    n_slab = SB_QB * SB_GROUPS
    return pl.pallas_call(
        _sb_kernel,
        grid=(B, n_q),
        in_specs=[pl.BlockSpec((rows, width), lambda b, i: (b * n_q + i, 0)),
                  pl.BlockSpec((S, width), lambda b, i: (b, 0)),
                  pl.BlockSpec((S, width), lambda b, i: (b, 0)),
                  pl.BlockSpec((2 * SB_T, 2 * SB_T), lambda b, i: (0, 0)),
                  pl.BlockSpec((1, width), lambda b, i: (0, 0))],
        out_specs=pl.BlockSpec((rows, width), lambda b, i: (b * n_q + i, 0)),
        out_shape=jax.ShapeDtypeStruct((B * S, width), BF16),
        scratch_shapes=[pltpu.VMEM((n_slab, 2 * SB_T, SB_GW), F32),
                        pltpu.VMEM((SB_TILES, SB_T, SB_T), F32),
                        pltpu.VMEM((n_slab, 2 * SB_T, SB_GW), BF16),
                        pltpu.VMEM((n_slab, 2 * SB_T, SB_GW), BF16),
                        pltpu.VMEM((SB_TILES, SB_T, SB_T), F32),
                        pltpu.VMEM((SB_TILES * SB_T, 2 * SB_T), BF16),
                        pltpu.VMEM((SB_TILES * SB_T, 2 * SB_T), F32)],
        compiler_params=pltpu.CompilerParams(dimension_semantics=("arbitrary", "arbitrary"),
                                             vmem_limit_bytes=VMEM_LIMIT),
        name="sb",
    )(qs, ks, vs, tu, gn)


def _layer_norm(y, g, b):
    mu = jnp.mean(y, axis=-1, keepdims=True)
    d = y - mu
    var = jnp.mean(d * d, axis=-1, keepdims=True)
    return d * lax.rsqrt(var + EPS) * g + b


def _outproj_kernel(og_ref, os_ref, x_ref, wo_ref, g_ref, b_ref, x1_ref, x1b_ref):
    half = og_ref.shape[1]
    mix = _dot(og_ref[...], wo_ref[:half, :]) + _dot(os_ref[...], wo_ref[half:, :])
    x1 = _layer_norm(ALPHA * x_ref[...] + mix, g_ref[...], b_ref[...])
    x1_ref[...] = x1
    x1b_ref[...] = x1.astype(BF16)


def _outproj(ogla, osb, x2d, wo, g, b, tm):
    T, D = x2d.shape
    half = ogla.shape[1]
    return pl.pallas_call(
        _outproj_kernel,
        grid=(T // tm,),
        in_specs=[pl.BlockSpec((tm, half), lambda i: (i, 0)),
                  pl.BlockSpec((tm, half), lambda i: (i, 0)),
                  pl.BlockSpec((tm, D), lambda i: (i, 0)),
                  pl.BlockSpec((2 * half, D), lambda i: (0, 0)),
                  pl.BlockSpec((1, D), lambda i: (0, 0)),
                  pl.BlockSpec((1, D), lambda i: (0, 0))],
        out_specs=[pl.BlockSpec((tm, D), lambda i: (i, 0)),
                   pl.BlockSpec((tm, D), lambda i: (i, 0))],
        out_shape=[jax.ShapeDtypeStruct((T, D), F32), jax.ShapeDtypeStruct((T, D), BF16)],
        compiler_params=pltpu.CompilerParams(dimension_semantics=("arbitrary",),
                                             vmem_limit_bytes=VMEM_LIMIT),
        name="outproj",
    )(ogla, osb, x2d, wo, g, b)


PEER_SEL_HEADS_PER_BODY = 8
SUBLANES = 8
PACKED_ROWS = 16


def _batcher_network(n):
    pairs, p = [], 1
    while p < n:
        k = p
        while k >= 1:
            for j in range(k % p, n - k, 2 * k):
                for i in range(min(k, n - j - k)):
                    if (i + j) // (2 * p) == (i + j + k) // (2 * p):
                        pairs.append((i + j, i + j + k))
            k //= 2
        p *= 2
    return pairs


def _merge_top(stacks, n):
    stacks = [list(st) for st in stacks]
    tops = []
    for r in range(n):
        head = stacks[0][0]
        for st in stacks[1:]:
            head = jnp.maximum(head, st[0])
        m = jnp.max(head, axis=0, keepdims=True)
        tops.append(m)
        need = n - r - 1
        for st in stacks:
            if need == 0:
                break
            hit = st[0] == m
            depth = min(need, len(st))
            for d in range(depth):
                below = st[d + 1] if d + 1 < len(st) else NEG_INF
                st[d] = jnp.where(hit, below, st[d])
            del st[depth:]
    return tops


def _top_rows(s, n):
    slabs = [s[r:r + SUBLANES] for r in range(0, s.shape[0], SUBLANES)]
    for i, j in _batcher_network(len(slabs)):
        slabs[i], slabs[j] = jnp.maximum(slabs[i], slabs[j]), jnp.minimum(slabs[i], slabs[j])
    return _merge_top([slabs], n)


def _peer_sel_kernel(x_ref, wq_ref, sk_ref, s1_ref, e1_ref, th_ref, c_ref, q_scr):
    tt = x_ref.shape[0]
    qry = _dot(x_ref[...], wq_ref[...]).astype(BF16)
    for hp in range(2 * PEER_HEADS):
        q_scr[hp] = qry[:, hp * PEER_HALF:(hp + 1) * PEER_HALF]
    sub = lax.broadcasted_iota(jnp.int32, (SUBLANES, tt), 0)

    def head(h):
        s0 = _dot_nt(sk_ref[2 * h], q_scr[2 * h])
        s1 = _dot_nt(sk_ref[2 * h + 1], q_scr[2 * h + 1])
        n = PEER_TOPK + 1
        a = _top_rows(s0, n)
        b = _top_rows(s1, n)
        a_lo = jnp.concatenate(a[:SUBLANES], axis=0)
        a_hi = jnp.concatenate(a[SUBLANES:2 * SUBLANES], axis=0)
        xs = [jnp.where(sub < n // l, a_lo + b[l - 1], NEG_INF) for l in range(1, n + 1)]
        zs = [a_hi + b[0], jnp.where(sub < 1, a[n - 1] + b[0], NEG_INF)]
        best = _merge_top([xs, zs], n)
        z = jnp.zeros_like(best[0])
        for r in range(PEER_TOPK):
            z = z + jnp.exp(best[r] - best[0])
        tau = 0.5 * (best[PEER_TOPK - 1] + best[PEER_TOPK])
        d1 = s1 - b[0]
        d0 = s0 - a[0]
        tau_rel = tau - best[0]
        scale = PEER_QONE / jnp.maximum(-tau_rel, 1e-30)
        s1_ref[0, h] = jnp.maximum(d1 * scale, -2.0 * PEER_QONE).astype(jnp.int32).astype(jnp.int16)
        th_ref[0, h] = jnp.minimum((tau_rel - d0) * scale, 2.0 * PEER_QONE - 1.0).astype(jnp.int32)
        e1_ref[0, h] = jnp.exp(d1).astype(BF16)
        c_ref[0, h] = jnp.exp(d0) * (2.0 ** -0.5 / z)

    def head_group(i, _):
        for n in range(PEER_SEL_HEADS_PER_BODY):
            head(PEER_SEL_HEADS_PER_BODY * i + n)
        return 0

    lax.fori_loop(0, PEER_HEADS // PEER_SEL_HEADS_PER_BODY, head_group, 0)


def _peer_sel(x1b, wq, sk, tt):
    T, D = x1b.shape
    qd = wq.shape[1]
    shp = [jax.ShapeDtypeStruct((T // tt, PEER_HEADS, PEER_NKEYS, tt), dt)
           for dt in (jnp.int16, BF16, jnp.int32, F32)]
    ospec = pl.BlockSpec((1, PEER_HEADS, PEER_NKEYS, tt), lambda i: (i, 0, 0, 0))
    return pl.pallas_call(
        _peer_sel_kernel,
        grid=(T // tt,),
        in_specs=[pl.BlockSpec((tt, D), lambda i: (i, 0)),
                  pl.BlockSpec((D, qd), lambda i: (0, 0)),
                  pl.BlockSpec((2 * PEER_HEADS, PEER_NKEYS, PEER_HALF), lambda i: (0, 0, 0))],
        out_specs=[ospec, ospec, ospec, ospec],
        out_shape=shp,
        scratch_shapes=[pltpu.VMEM((2 * PEER_HEADS, tt, PEER_HALF), BF16)],
        compiler_params=pltpu.CompilerParams(dimension_semantics=("arbitrary",),
                                             vmem_limit_bytes=VMEM_LIMIT),
        name="peer_sel",
    )(x1b, wq, sk)


PEER_ET = 2048
PEER_EQ = 512
PEER_LC = 256


def _peer_ffn_kernel(xt_ref, x1_ref, u_ref, vt_ref, s1_ref, e1_ref, th_ref, c_ref, g_ref, b_ref,
                     o_ref, acc_ref, h_ref, p_ref):
    e = pl.program_id(1)
    n_lc = xt_ref.shape[0]
    n_q = PEER_ET // PEER_EQ
    n_k = PEER_EQ // PEER_NKEYS
    assert n_lc == 2

    @pl.when(e == 0)
    def _():
        acc_ref[...] = jnp.zeros_like(acc_ref)

    def scores(q, lc):
        r0 = q * PEER_EQ if isinstance(q, int) else pl.multiple_of(q * PEER_EQ, PEER_EQ)
        h_ref[lc] = _dot(u_ref[pl.ds(r0, PEER_EQ), :], xt_ref[lc])

    def gated_act(q, lc):
        reps = PEER_NKEYS // PACKED_ROWS
        for k in range(n_k):
            rows = slice(k * PEER_NKEYS, (k + 1) * PEER_NKEYS)
            gate = jnp.zeros((PEER_NKEYS, PEER_LC), BF16)
            for h in range(PEER_HEADS):
                th = th_ref[lc, h, pl.ds(q * n_k + k, 1), :]
                cc = c_ref[lc, h, pl.ds(q * n_k + k, 1), :]
                th = jnp.tile(jnp.broadcast_to(th, (PACKED_ROWS, PEER_LC)).astype(jnp.int16), (reps, 1))
                cc = jnp.tile(jnp.broadcast_to(cc, (PACKED_ROWS, PEER_LC)).astype(BF16), (reps, 1))
                val = e1_ref[lc, h] * cc
                gate = gate + jnp.where(s1_ref[lc, h] >= th, val, jnp.zeros_like(val))
            hh = h_ref[lc, rows, :]
            act = hh + hh * lax.erf(hh)
            p_ref[lc, rows, :] = gate * act.astype(BF16)

    def accumulate(q, lc):
        acc_ref[lc] += _dot(vt_ref[q], p_ref[lc])

    scores(0, 0)
    scores(0, 1)
    gated_act(0, 0)

    def step(q, _):
        scores(q + 1, 0)
        gated_act(q, 1)
        accumulate(q, 0)
        scores(q + 1, 1)
        gated_act(q + 1, 0)
        accumulate(q, 1)
        return 0

    lax.fori_loop(0, n_q - 1, step, 0)
    gated_act(n_q - 1, 1)
    accumulate(n_q - 1, 0)
    accumulate(n_q - 1, 1)

    @pl.when(e == pl.num_programs(1) - 1)
    def _():
        for lc in range(n_lc):
            rows = slice(lc * PEER_LC, (lc + 1) * PEER_LC)
            y = ALPHA * x1_ref[rows, :] + acc_ref[lc].T
            o_ref[rows, :] = _layer_norm(y, g_ref[...], b_ref[...])


def _peer_ffn(xt, x1, u, vt, s1, e1, th, cc, g, b, tt):
    T, D = x1.shape
    E = u.shape[0]
    n_lc = tt // PEER_LC
    n_q = PEER_ET // PEER_EQ
    tok = pl.BlockSpec((n_lc, PEER_HEADS, PEER_NKEYS, PEER_LC), lambda t, e: (t, 0, 0, 0))
    sel = pl.BlockSpec((n_lc, PEER_HEADS, PEER_ET // PEER_NKEYS, PEER_LC), lambda t, e: (t, 0, e, 0))
    return pl.pallas_call(
        _peer_ffn_kernel,
        grid=(T // tt, E // PEER_ET),
        in_specs=[pl.BlockSpec((n_lc, D, PEER_LC), lambda t, e: (t, 0, 0)),
                  pl.BlockSpec((tt, D), lambda t, e: (t, 0)),
                  pl.BlockSpec((PEER_ET, D), lambda t, e: (e, 0)),
                  pl.BlockSpec((n_q, D, PEER_EQ), lambda t, e: (e, 0, 0)),
                  tok, tok, sel, sel,
                  pl.BlockSpec((1, D), lambda t, e: (0, 0)),
                  pl.BlockSpec((1, D), lambda t, e: (0, 0))],
        out_specs=pl.BlockSpec((tt, D), lambda t, e: (t, 0)),
        out_shape=jax.ShapeDtypeStruct((T, D), F32),
        scratch_shapes=[pltpu.VMEM((n_lc, D, PEER_LC), F32),
                        pltpu.VMEM((n_lc, PEER_EQ, PEER_LC), F32),
                        pltpu.VMEM((n_lc, PEER_EQ, PEER_LC), BF16)],
        compiler_params=pltpu.CompilerParams(dimension_semantics=("arbitrary", "arbitrary"),
                                             vmem_limit_bytes=VMEM_LIMIT),
        name="peer_ffn",
    )(xt, x1, u, vt, s1, e1, th, cc, g, b)


def _layer(x, w_in, w_gla_gate, b_gla_gate, gla_norm_g, sb_norm_g, w_out, ln1_g, ln1_b,
           peer_w_query, peer_sub_keys, peer_u, peer_v, ln2_g, ln2_b):
    B, S, D = x.shape
    T = B * S
    x2d = x.reshape(T, D)

    kw, gw, sw = GLA_HEADS * GLA_DK, GLA_HEADS * GLA_DV, SB_HEADS * SB_DH
    pts = [0, kw, 2 * kw, 2 * kw + gw, 2 * kw + 2 * gw, 2 * kw + 2 * gw + GLA_RANK]
    pts += [pts[-1] + sw, pts[-1] + 2 * sw, pts[-1] + 3 * sw]
    cols = [w_in[:, pts[n]:pts[n + 1]] for n in range(8)]
    cols[4] = jnp.pad(cols[4], ((0, 0), (0, LANES - GLA_RANK)))
    w_cat = jnp.concatenate(cols, axis=1).astype(BF16)
    wg = jnp.pad(w_gla_gate, ((0, LANES - GLA_RANK), (0, 0))).astype(BF16)

    qg, kg, vg, gg, alr, qs, ks, vs = _inproj(x2d, w_cat, min(TOKEN_TILE, T))

    ogla = _gla(qg, kg, vg, gg, alr, wg, b_gla_gate.reshape(1, -1), gla_norm_g.reshape(1, -1),
                B, S, min(GLA_SEQ_BLOCK, S))

    r = lax.broadcasted_iota(jnp.int32, (2 * SB_T, 2 * SB_T), 0) % SB_T
    c = lax.broadcasted_iota(jnp.int32, (2 * SB_T, 2 * SB_T), 1)
    tu = jnp.where((c >= SB_T) | (r > c), 1.0, 0.0).astype(BF16)
    osb = _sb(qs, ks, vs, tu, sb_norm_g.reshape(1, -1), B, S)

    x1, x1b = _outproj(ogla, osb, x2d, w_out.astype(BF16), ln1_g.reshape(1, D), ln1_b.reshape(1, D),
                       min(TOKEN_TILE, T))

    sk = peer_sub_keys.reshape(2 * PEER_HEADS, PEER_NKEYS, PEER_HALF).astype(BF16)
    s1, e1, th, cc = _peer_sel(x1b, peer_w_query.astype(BF16), sk, PEER_LC)

    xt = x1b.reshape(-1, PEER_LC, D).transpose(0, 2, 1)
    vt = peer_v.astype(BF16).reshape(-1, PEER_EQ, D).transpose(0, 2, 1)
    out = _peer_ffn(xt, x1, (peer_u * 2.0 ** -0.5).astype(BF16), vt, s1, e1, th, cc,
                    ln2_g.reshape(1, D), ln2_b.reshape(1, D), 2 * PEER_LC)
    return out.reshape(B, S, D)


def kernel(x, w_in, w_gla_gate, b_gla_gate, gla_norm_g, sb_norm_g, w_out, ln1_g, ln1_b,
           peer_w_query, peer_sub_keys, peer_u, peer_v, ln2_g, ln2_b):
    for l in range(DEPTH):
        x = _layer(x, w_in[l], w_gla_gate[l], b_gla_gate[l], gla_norm_g[l], sb_norm_g[l], w_out[l],
                   ln1_g[l], ln1_b[l], peer_w_query[l], peer_sub_keys[l], peer_u[l], peer_v[l],
                   ln2_g[l], ln2_b[l])
    return x
```

```python
import functools

import jax
import jax.numpy as jnp
from jax import lax
from jax.experimental import pallas as pl
from jax.experimental.pallas import tpu as pltpu

F32 = jnp.float32
BF16 = jnp.bfloat16

LANES = 128
GLA_HEADS = 4
GLA_DK = 64
GLA_DV = 128
GLA_RANK = 16
GLA_GATE_TEMP = 16.0
GLA_CHUNK = 64
SB_HEADS = 8
SB_DH = 64
PEER_HEADS = 8
PEER_NKEYS = 128
PEER_HALF = 128
PEER_TOPK = 16
PEER_QONE = 16384.0
DEPTH = 1
ALPHA = (2.0 * DEPTH) ** 0.25
EPS = 1e-5
NEG_INF = float("-inf")
SB_DEAD = -104.0

VMEM_CAPACITY_V7X = 64 * 1024 * 1024
VMEM_LIMIT = VMEM_CAPACITY_V7X * 7 // 8
TOKEN_TILE = 512
GLA_SEQ_BLOCK = 512

def _dot(a, b):
    return jnp.dot(a, b, preferred_element_type=F32)


def _dot_nt(a, b):
    return lax.dot_general(a, b, (((1,), (1,)), ((), ())), preferred_element_type=F32)


def _dot_tn(a, b):
    return lax.dot_general(a, b, (((0,), (0,)), ((), ())), preferred_element_type=F32)


def _split_bf16(v):
    hi = v.astype(BF16)
    lo = (v - hi.astype(F32)).astype(BF16)
    return hi, lo


def _log_sigmoid(z):
    return jnp.minimum(z, 0.0) - jnp.log(1.0 + jnp.exp(-jnp.abs(z)))


IN_SPLITS = (("qg", 256, F32), ("kg", 256, F32), ("vg", 512, BF16), ("gg", 512, F32),
             ("alr", LANES, F32), ("qs", 512, BF16), ("ks", 512, BF16), ("vs", 512, BF16))


def _inproj_kernel(x_ref, w_ref, *out_refs):
    xb = x_ref[...].astype(BF16)
    off = 0
    for (name, width, _), o_ref in zip(IN_SPLITS, out_refs):
        r = _dot(xb, w_ref[:, off:off + width])
        if name in ("qg", "qs"):
            r = r * (GLA_DK ** -0.5)
        o_ref[...] = r.astype(o_ref.dtype)
        off += width


def _inproj(x2d, w_cat, tm):
    T, D = x2d.shape
    wtot = w_cat.shape[1]
    return pl.pallas_call(
        _inproj_kernel,
        grid=(T // tm,),
        in_specs=[pl.BlockSpec((tm, D), lambda i: (i, 0)),
                  pl.BlockSpec((D, wtot), lambda i: (0, 0))],
        out_specs=[pl.BlockSpec((tm, w), lambda i: (i, 0)) for _, w, _ in IN_SPLITS],
        out_shape=[jax.ShapeDtypeStruct((T, w), dt) for _, w, dt in IN_SPLITS],
        compiler_params=pltpu.CompilerParams(dimension_semantics=("arbitrary",),
                                             vmem_limit_bytes=VMEM_LIMIT),
        name="inproj",
    )(x2d, w_cat)


def _gla_kernel(q_ref, k_ref, v_ref, g_ref, a_ref, wg_ref, bg_ref, gn_ref, o_ref, st_ref, *, n_chunks):
    @pl.when(pl.program_id(0) == 0)
    def _():
        st_ref[...] = jnp.zeros_like(st_ref)

    C = GLA_CHUNK
    KW, VW = 2 * GLA_DK, 2 * GLA_DV
    lane = lax.broadcasted_iota(jnp.int32, (1, KW), 1)
    head_mask = [lane < GLA_DK, lane >= GLA_DK]
    r_i = lax.broadcasted_iota(jnp.int32, (C, C), 0)
    c_i = lax.broadcasted_iota(jnp.int32, (C, C), 1)
    causal = c_i <= r_i
    tri = jnp.where(causal, 1.0, 0.0).astype(BF16)

    n_batch = q_ref.shape[0]
    la_all = [_log_sigmoid(_dot(a_ref[bi].astype(BF16), wg_ref[...]) + bg_ref[...]) * (1.0 / GLA_GATE_TEMP)
              for bi in range(n_batch)]

    for c in range(n_chunks):
        rows = slice(c * C, (c + 1) * C)
        for bi in range(n_batch):
            la = la_all[bi][rows]
            hi, lo = _split_bf16(la)
            b_all = _dot(tri, hi) + _dot(tri, lo)
            for hp in range(GLA_HEADS // 2):
                kcols = slice(hp * KW, (hp + 1) * KW)
                vcols = slice(hp * VW, (hp + 1) * VW)
                b = b_all[:, kcols]
                b_last = b[C - 1:C, :]
                q = q_ref[bi, rows, kcols]
                k = k_ref[bi, rows, kcols]
                q_in = q * jnp.exp(b)
                k_in = (k * jnp.exp(-b)).astype(BF16)
                k_st = (k * jnp.exp(b_last - b)).astype(BF16)
                dec = jnp.exp(b_last)
                v = v_ref[bi, rows, vcols]
                st = st_ref[bi, hp]
                st_b = st.astype(BF16)
                outs = []
                for h in range(2):
                    qm = jnp.where(head_mask[h], q_in, 0.0).astype(BF16)
                    att = jnp.where(causal, _dot_nt(qm, k_in), 0.0)
                    vh = v[:, h * GLA_DV:(h + 1) * GLA_DV]
                    o_h = _dot(att.astype(BF16), vh) + _dot_nt(qm, st_b[h * GLA_DV:(h + 1) * GLA_DV, :])
                    ms = jnp.mean(o_h * o_h, axis=-1, keepdims=True)
                    outs.append(o_h * lax.rsqrt(ms + EPS))
                st_ref[bi, hp] = st * dec + _dot_tn(v, k_st)
                o = jnp.concatenate(outs, axis=1) * gn_ref[:, vcols]
                g = g_ref[bi, rows, vcols]
                o_ref[bi, rows, vcols] = (o * (g * jax.nn.sigmoid(g))).astype(o_ref.dtype)


def _gla(qg, kg, vg, gg, alr, wg, bg, gn, B, S, ls):
    kw, vw = GLA_HEADS * GLA_DK, GLA_HEADS * GLA_DV
    seq = lambda s: (0, s, 0)
    par = lambda s: (0, 0)
    per_batch = lambda t: t.reshape(B, S, t.shape[-1])
    out = pl.pallas_call(
        functools.partial(_gla_kernel, n_chunks=ls // GLA_CHUNK),
        grid=(S // ls,),
        in_specs=[pl.BlockSpec((B, ls, kw), seq),
                  pl.BlockSpec((B, ls, kw), seq),
                  pl.BlockSpec((B, ls, vw), seq),
                  pl.BlockSpec((B, ls, vw), seq),
                  pl.BlockSpec((B, ls, LANES), seq),
                  pl.BlockSpec((LANES, kw), par),
                  pl.BlockSpec((1, kw), par),
                  pl.BlockSpec((1, vw), par)],
        out_specs=pl.BlockSpec((B, ls, vw), seq),
        out_shape=jax.ShapeDtypeStruct((B, S, vw), BF16),
        scratch_shapes=[pltpu.VMEM((B, GLA_HEADS // 2, 2 * GLA_DV, 2 * GLA_DK), F32)],
        compiler_params=pltpu.CompilerParams(dimension_semantics=("arbitrary",),
                                             vmem_limit_bytes=VMEM_LIMIT),
        name="gla",
    )(per_batch(qg), per_batch(kg), per_batch(vg), per_batch(gg), per_batch(alr), wg, bg, gn)
    return out.reshape(B * S, vw)


SB_T = 128
SB_QB = 4
SB_GW = 2 * LANES
SB_GROUPS = SB_HEADS * SB_DH // SB_GW
SB_TILES = 4 * SB_GROUPS * SB_QB


def _sb_kernel(q_ref, k_ref, v_ref, tu_ref, gn_ref, o_ref,
               acc_ref, car_ref, kb_ref, vb_ref, zl_ref, l_ref, cs_ref):
    i = pl.program_id(1)
    T = SB_T
    lane = lax.broadcasted_iota(jnp.int32, (1, SB_GW), 1)
    is_h0 = (lane % LANES) < SB_DH
    qg = []
    for a in range(SB_QB):
        qa = []
        for g in range(SB_GROUPS):
            q = q_ref[a * T:(a + 1) * T, g * SB_GW:(g + 1) * SB_GW]
            zero = jnp.zeros_like(q)
            qa.append(jnp.concatenate([jnp.where(is_h0, q, zero), jnp.where(is_h0, zero, q)], axis=0))
        qg.append(qa)
    row = lax.broadcasted_iota(jnp.int32, (T, T), 0)
    col = lax.broadcasted_iota(jnp.int32, (T, T), 1)
    acc_ref[...] = jnp.zeros_like(acc_ref)
    car_ref[...] = jnp.zeros_like(car_ref)
    kb_ref[...] = jnp.zeros_like(kb_ref)
    vb_ref[...] = jnp.zeros_like(vb_ref)

    def sweep(m, diagonal):
        valid = col < row
        for a in range(SB_QB):
            j = SB_QB * i + a - m
            ks = pl.multiple_of(jnp.maximum(j, 0) * T, T)
            for g in range(SB_GROUPS):
                s = a * SB_GROUPS + g
                for p in range(2):
                    src = slice(g * SB_GW + p * LANES, g * SB_GW + (p + 1) * LANES)
                    kb_ref[s, p * T:(p + 1) * T, p * LANES:(p + 1) * LANES] = k_ref[pl.ds(ks, T), src]
                    vb_ref[s, p * T:(p + 1) * T, p * LANES:(p + 1) * LANES] = v_ref[pl.ds(ks, T), src]
                z = _dot_nt(qg[a][g], kb_ref[s])
                for h in range(2):
                    for p in range(2):
                        n = 4 * s + 2 * h + p
                        zq = z[h * T:(h + 1) * T, p * T:(p + 1) * T]
                        lnb = -(jnp.maximum(zq, 0.0) + jnp.log(1.0 + jnp.exp(-jnp.abs(zq))))
                        if diagonal:
                            lnb = jnp.where(valid, lnb, 0.0)
                        hi, lo = _split_bf16(lnb)
                        l_ref[n * T:(n + 1) * T, :T] = hi
                        l_ref[n * T:(n + 1) * T, T:] = lo
                        zl_ref[n] = zq + lnb
        cs_ref[...] = _dot(l_ref[...], tu_ref[...])
        top = None
        for a in range(SB_QB):
            j = SB_QB * i + a - m
            spent = jnp.where(j < 0, -1e30, 0.0).astype(F32)
            for g in range(SB_GROUPS):
                s = a * SB_GROUPS + g
                w_rows = []
                for h in range(2):
                    w_cols = []
                    for p in range(2):
                        n = 4 * s + 2 * h + p
                        car = car_ref[n] if diagonal else car_ref[n] + spent
                        w = jnp.exp(zl_ref[n] + cs_ref[n * T:(n + 1) * T, :T] + car)
                        if diagonal:
                            w = jnp.where(valid, w, 0.0)
                        w_cols.append(w.astype(BF16))
                        car = car + cs_ref[n * T:(n + 1) * T, T:]
                        car_ref[n] = car
                        top = car if top is None else jnp.maximum(top, car)
                    w_rows.append(jnp.concatenate(w_cols, axis=1))
                acc_ref[s] += _dot(jnp.concatenate(w_rows, axis=0), vb_ref[s])
        return jnp.max(top)

    def cond(carry):
        m, alive = carry
        return jnp.logical_and(m <= SB_QB * i + SB_QB - 1, alive > SB_DEAD)

    def body(carry):
        m, _ = carry
        return m + 1, sweep(m, diagonal=False)

    lax.while_loop(cond, body, (jnp.int32(1), sweep(0, diagonal=True)))

    head_of_lane = lane // SB_DH
    for a in range(SB_QB):
        for g in range(SB_GROUPS):
            acc = acc_ref[a * SB_GROUPS + g]
            o = jnp.where(is_h0, acc[:T], acc[T:])
            sq = o * o
            ms = jnp.zeros_like(o)
            for hd in range(SB_GW // SB_DH):
                mine = head_of_lane == hd
                ms = jnp.where(mine, jnp.sum(jnp.where(mine, sq, 0.0), axis=-1, keepdims=True), ms)
            cols = slice(g * SB_GW, (g + 1) * SB_GW)
            o_ref[a * T:(a + 1) * T, cols] = (o * lax.rsqrt(ms * (1.0 / SB_DH) + EPS) * gn_ref[:, cols]).astype(o_ref.dtype)


def _sb(qs, ks, vs, tu, gn, B, S):
    rows = SB_QB * SB_T
    n_q = S // rows
    width = SB_HEADS * SB_DH
    n_slab = SB_QB * SB_GROUPS
    return pl.pallas_call(
        _sb_kernel,
        grid=(B, n_q),
        in_specs=[pl.BlockSpec((rows, width), lambda b, i: (b * n_q + i, 0)),
                  pl.BlockSpec((S, width), lambda b, i: (b, 0)),
                  pl.BlockSpec((S, width), lambda b, i: (b, 0)),
                  pl.BlockSpec((2 * SB_T, 2 * SB_T), lambda b, i: (0, 0)),
                  pl.BlockSpec((1, width), lambda b, i: (0, 0))],
        out_specs=pl.BlockSpec((rows, width), lambda b, i: (b * n_q + i, 0)),
        out_shape=jax.ShapeDtypeStruct((B * S, width), BF16),
        scratch_shapes=[pltpu.VMEM((n_slab, 2 * SB_T, SB_GW), F32),
                        pltpu.VMEM((SB_TILES, SB_T, SB_T), F32),
                        pltpu.VMEM((n_slab, 2 * SB_T, SB_GW), BF16),
                        pltpu.VMEM((n_slab, 2 * SB_T, SB_GW), BF16),
                        pltpu.VMEM((SB_TILES, SB_T, SB_T), F32),
                        pltpu.VMEM((SB_TILES * SB_T, 2 * SB_T), BF16),
                        pltpu.VMEM((SB_TILES * SB_T, 2 * SB_T), F32)],
        compiler_params=pltpu.CompilerParams(dimension_semantics=("arbitrary", "arbitrary"),
                                             vmem_limit_bytes=VMEM_LIMIT),
        name="sb",
    )(qs, ks, vs, tu, gn)


def _layer_norm(y, g, b):
    mu = jnp.mean(y, axis=-1, keepdims=True)
    d = y - mu
    var = jnp.mean(d * d, axis=-1, keepdims=True)
    return d * lax.rsqrt(var + EPS) * g + b


def _outproj_kernel(og_ref, os_ref, x_ref, wo_ref, g_ref, b_ref, x1_ref, x1b_ref):
    half = og_ref.shape[1]
    mix = _dot(og_ref[...], wo_ref[:half, :]) + _dot(os_ref[...], wo_ref[half:, :])
    x1 = _layer_norm(ALPHA * x_ref[...] + mix, g_ref[...], b_ref[...])
    x1_ref[...] = x1
    x1b_ref[...] = x1.astype(BF16)


def _outproj(ogla, osb, x2d, wo, g, b, tm):
    T, D = x2d.shape
    half = ogla.shape[1]
    return pl.pallas_call(
        _outproj_kernel,
        grid=(T // tm,),
        in_specs=[pl.BlockSpec((tm, half), lambda i: (i, 0)),
                  pl.BlockSpec((tm, half), lambda i: (i, 0)),
                  pl.BlockSpec((tm, D), lambda i: (i, 0)),
                  pl.BlockSpec((2 * half, D), lambda i: (0, 0)),
                  pl.BlockSpec((1, D), lambda i: (0, 0)),
                  pl.BlockSpec((1, D), lambda i: (0, 0))],
        out_specs=[pl.BlockSpec((tm, D), lambda i: (i, 0)),
                   pl.BlockSpec((tm, D), lambda i: (i, 0))],
        out_shape=[jax.ShapeDtypeStruct((T, D), F32), jax.ShapeDtypeStruct((T, D), BF16)],
        compiler_params=pltpu.CompilerParams(dimension_semantics=("arbitrary",),
                                             vmem_limit_bytes=VMEM_LIMIT),
        name="outproj",
    )(ogla, osb, x2d, wo, g, b)


PEER_SEL_HEADS_PER_BODY = 8
SUBLANES = 8
PACKED_ROWS = 16


def _batcher_network(n):
    pairs, p = [], 1
    while p < n:
        k = p
        while k >= 1:
            for j in range(k % p, n - k, 2 * k):
                for i in range(min(k, n - j - k)):
                    if (i + j) // (2 * p) == (i + j + k) // (2 * p):
                        pairs.append((i + j, i + j + k))
            k //= 2
        p *= 2
    return pairs


def _merge_top(stacks, n):
    stacks = [list(st) for st in stacks]
    tops = []
    for r in range(n):
        head = stacks[0][0]
        for st in stacks[1:]:
            head = jnp.maximum(head, st[0])
        m = jnp.max(head, axis=0, keepdims=True)
        tops.append(m)
        need = n - r - 1
        for st in stacks:
            if need == 0:
                break
            hit = st[0] == m
            depth = min(need, len(st))
            for d in range(depth):
                below = st[d + 1] if d + 1 < len(st) else NEG_INF
                st[d] = jnp.where(hit, below, st[d])
            del st[depth:]
    return tops


def _top_rows(s, n):
    slabs = [s[r:r + SUBLANES] for r in range(0, s.shape[0], SUBLANES)]
    for i, j in _batcher_network(len(slabs)):
        slabs[i], slabs[j] = jnp.maximum(slabs[i], slabs[j]), jnp.minimum(slabs[i], slabs[j])
    return _merge_top([slabs], n)


def _peer_sel_kernel(x_ref, wq_ref, sk_ref, s1_ref, e1_ref, th_ref, c_ref, q_scr):
    tt = x_ref.shape[0]
    qry = _dot(x_ref[...], wq_ref[...]).astype(BF16)
    for hp in range(2 * PEER_HEADS):
        q_scr[hp] = qry[:, hp * PEER_HALF:(hp + 1) * PEER_HALF]
    sub = lax.broadcasted_iota(jnp.int32, (SUBLANES, tt), 0)

    def head(h):
        s0 = _dot_nt(sk_ref[2 * h], q_scr[2 * h])
        s1 = _dot_nt(sk_ref[2 * h + 1], q_scr[2 * h + 1])
        n = PEER_TOPK + 1
        a = _top_rows(s0, n)
        b = _top_rows(s1, n)
        a_lo = jnp.concatenate(a[:SUBLANES], axis=0)
        a_hi = jnp.concatenate(a[SUBLANES:2 * SUBLANES], axis=0)
        xs = [jnp.where(sub < n // l, a_lo + b[l - 1], NEG_INF) for l in range(1, n + 1)]
        zs = [a_hi + b[0], jnp.where(sub < 1, a[n - 1] + b[0], NEG_INF)]
        best = _merge_top([xs, zs], n)
        z = jnp.zeros_like(best[0])
        for r in range(PEER_TOPK):
            z = z + jnp.exp(best[r] - best[0])
        tau = 0.5 * (best[PEER_TOPK - 1] + best[PEER_TOPK])
        d1 = s1 - b[0]
        d0 = s0 - a[0]
        tau_rel = tau - best[0]
        scale = PEER_QONE / jnp.maximum(-tau_rel, 1e-30)
        s1_ref[0, h] = jnp.maximum(d1 * scale, -2.0 * PEER_QONE).astype(jnp.int32).astype(jnp.int16)
        th_ref[0, h] = jnp.minimum((tau_rel - d0) * scale, 2.0 * PEER_QONE - 1.0).astype(jnp.int32)
        e1_ref[0, h] = jnp.exp(d1).astype(BF16)
        c_ref[0, h] = jnp.exp(d0) * (2.0 ** -0.5 / z)

    def head_group(i, _):
        for n in range(PEER_SEL_HEADS_PER_BODY):
            head(PEER_SEL_HEADS_PER_BODY * i + n)
        return 0

    lax.fori_loop(0, PEER_HEADS // PEER_SEL_HEADS_PER_BODY, head_group, 0)


def _peer_sel(x1b, wq, sk, tt):
    T, D = x1b.shape
    qd = wq.shape[1]
    shp = [jax.ShapeDtypeStruct((T // tt, PEER_HEADS, PEER_NKEYS, tt), dt)
           for dt in (jnp.int16, BF16, jnp.int32, F32)]
    ospec = pl.BlockSpec((1, PEER_HEADS, PEER_NKEYS, tt), lambda i: (i, 0, 0, 0))
    return pl.pallas_call(
        _peer_sel_kernel,
        grid=(T // tt,),
        in_specs=[pl.BlockSpec((tt, D), lambda i: (i, 0)),
                  pl.BlockSpec((D, qd), lambda i: (0, 0)),
                  pl.BlockSpec((2 * PEER_HEADS, PEER_NKEYS, PEER_HALF), lambda i: (0, 0, 0))],
        out_specs=[ospec, ospec, ospec, ospec],
        out_shape=shp,
        scratch_shapes=[pltpu.VMEM((2 * PEER_HEADS, tt, PEER_HALF), BF16)],
        compiler_params=pltpu.CompilerParams(dimension_semantics=("arbitrary",),
                                             vmem_limit_bytes=VMEM_LIMIT),
        name="peer_sel",
    )(x1b, wq, sk)


PEER_ET = 2048
PEER_EQ = 512
PEER_LC = 256


def _peer_ffn_kernel(xt_ref, x1_ref, u_ref, vt_ref, s1_ref, e1_ref, th_ref, c_ref, g_ref, b_ref,
                     o_ref, acc_ref, h_ref, p_ref):
    e = pl.program_id(1)
    n_lc = xt_ref.shape[0]
    n_q = PEER_ET // PEER_EQ
    n_k = PEER_EQ // PEER_NKEYS
    assert n_lc == 2

    @pl.when(e == 0)
    def _():
        acc_ref[...] = jnp.zeros_like(acc_ref)

    def scores(q, lc):
        r0 = q * PEER_EQ if isinstance(q, int) else pl.multiple_of(q * PEER_EQ, PEER_EQ)
        h_ref[lc] = _dot(u_ref[pl.ds(r0, PEER_EQ), :], xt_ref[lc])

    def gated_act(q, lc):
        reps = PEER_NKEYS // PACKED_ROWS
        for k in range(n_k):
            rows = slice(k * PEER_NKEYS, (k + 1) * PEER_NKEYS)
            gate = jnp.zeros((PEER_NKEYS, PEER_LC), BF16)
            for h in range(PEER_HEADS):
                th = th_ref[lc, h, pl.ds(q * n_k + k, 1), :]
                cc = c_ref[lc, h, pl.ds(q * n_k + k, 1), :]
                th = jnp.tile(jnp.broadcast_to(th, (PACKED_ROWS, PEER_LC)).astype(jnp.int16), (reps, 1))
                cc = jnp.tile(jnp.broadcast_to(cc, (PACKED_ROWS, PEER_LC)).astype(BF16), (reps, 1))
                val = e1_ref[lc, h] * cc
                gate = gate + jnp.where(s1_ref[lc, h] >= th, val, jnp.zeros_like(val))
            hh = h_ref[lc, rows, :]
            act = hh + hh * lax.erf(hh)
            p_ref[lc, rows, :] = gate * act.astype(BF16)

    def accumulate(q, lc):
        r0 = q * PEER_EQ if isinstance(q, int) else pl.multiple_of(q * PEER_EQ, PEER_EQ)
        acc_ref[lc] += _dot_tn(vt_ref[pl.ds(r0, PEER_EQ), :], p_ref[lc])

    scores(0, 0)
    scores(0, 1)
    gated_act(0, 0)

    def step(q, _):
        scores(q + 1, 0)
        gated_act(q, 1)
        accumulate(q, 0)
        scores(q + 1, 1)
        gated_act(q + 1, 0)
        accumulate(q, 1)
        return 0

    lax.fori_loop(0, n_q - 1, step, 0)
    gated_act(n_q - 1, 1)
    accumulate(n_q - 1, 0)
    accumulate(n_q - 1, 1)

    @pl.when(e == pl.num_programs(1) - 1)
    def _():
        for lc in range(n_lc):
            rows = slice(lc * PEER_LC, (lc + 1) * PEER_LC)
            y = ALPHA * x1_ref[rows, :] + acc_ref[lc].T
            o_ref[rows, :] = _layer_norm(y, g_ref[...], b_ref[...])


def _peer_ffn(xt, x1, u, vt, s1, e1, th, cc, g, b, tt):
    T, D = x1.shape
    E = u.shape[0]
    n_lc = tt // PEER_LC
    n_q = PEER_ET // PEER_EQ
    tok = pl.BlockSpec((n_lc, PEER_HEADS, PEER_NKEYS, PEER_LC), lambda t, e: (t, 0, 0, 0))
    sel = pl.BlockSpec((n_lc, PEER_HEADS, PEER_ET // PEER_NKEYS, PEER_LC), lambda t, e: (t, 0, e, 0))
    return pl.pallas_call(
        _peer_ffn_kernel,
        grid=(T // tt, E // PEER_ET),
        in_specs=[pl.BlockSpec((n_lc, D, PEER_LC), lambda t, e: (t, 0, 0)),
                  pl.BlockSpec((tt, D), lambda t, e: (t, 0)),
                  pl.BlockSpec((PEER_ET, D), lambda t, e: (e, 0)),
                  pl.BlockSpec((PEER_ET, D), lambda t, e: (e, 0)),
                  tok, tok, sel, sel,
                  pl.BlockSpec((1, D), lambda t, e: (0, 0)),
                  pl.BlockSpec((1, D), lambda t, e: (0, 0))],
        out_specs=pl.BlockSpec((tt, D), lambda t, e: (t, 0)),
        out_shape=jax.ShapeDtypeStruct((T, D), F32),
        scratch_shapes=[pltpu.VMEM((n_lc, D, PEER_LC), F32),
                        pltpu.VMEM((n_lc, PEER_EQ, PEER_LC), F32),
                        pltpu.VMEM((n_lc, PEER_EQ, PEER_LC), BF16)],
        compiler_params=pltpu.CompilerParams(dimension_semantics=("arbitrary", "arbitrary"),
                                             vmem_limit_bytes=VMEM_LIMIT),
        name="peer_ffn",
    )(xt, x1, u, vt, s1, e1, th, cc, g, b)


def _layer(x, w_in, w_gla_gate, b_gla_gate, gla_norm_g, sb_norm_g, w_out, ln1_g, ln1_b,
           peer_w_query, peer_sub_keys, peer_u, peer_v, ln2_g, ln2_b):
    B, S, D = x.shape
    T = B * S
    x2d = x.reshape(T, D)

    kw, gw, sw = GLA_HEADS * GLA_DK, GLA_HEADS * GLA_DV, SB_HEADS * SB_DH
    pts = [0, kw, 2 * kw, 2 * kw + gw, 2 * kw + 2 * gw, 2 * kw + 2 * gw + GLA_RANK]
    pts += [pts[-1] + sw, pts[-1] + 2 * sw, pts[-1] + 3 * sw]
    cols = [w_in[:, pts[n]:pts[n + 1]] for n in range(8)]
    cols[4] = jnp.pad(cols[4], ((0, 0), (0, LANES - GLA_RANK)))
    w_cat = jnp.concatenate(cols, axis=1).astype(BF16)
    wg = jnp.pad(w_gla_gate, ((0, LANES - GLA_RANK), (0, 0))).astype(BF16)

    qg, kg, vg, gg, alr, qs, ks, vs = _inproj(x2d, w_cat, min(TOKEN_TILE, T))

    ogla = _gla(qg, kg, vg, gg, alr, wg, b_gla_gate.reshape(1, -1), gla_norm_g.reshape(1, -1),
                B, S, min(GLA_SEQ_BLOCK, S))

    r = lax.broadcasted_iota(jnp.int32, (2 * SB_T, 2 * SB_T), 0) % SB_T
    c = lax.broadcasted_iota(jnp.int32, (2 * SB_T, 2 * SB_T), 1)
    tu = jnp.where((c >= SB_T) | (r > c), 1.0, 0.0).astype(BF16)
    osb = _sb(qs, ks, vs, tu, sb_norm_g.reshape(1, -1), B, S)

    x1, x1b = _outproj(ogla, osb, x2d, w_out.astype(BF16), ln1_g.reshape(1, D), ln1_b.reshape(1, D),
                       min(TOKEN_TILE, T))

    sk = peer_sub_keys.reshape(2 * PEER_HEADS, PEER_NKEYS, PEER_HALF).astype(BF16)
    s1, e1, th, cc = _peer_sel(x1b, peer_w_query.astype(BF16), sk, PEER_LC)

    xt = x1b.reshape(-1, PEER_LC, D).transpose(0, 2, 1)
    vt = peer_v.astype(BF16)
    out = _peer_ffn(xt, x1, (peer_u * 2.0 ** -0.5).astype(BF16), vt, s1, e1, th, cc,
                    ln2_g.reshape(1, D), ln2_b.reshape(1, D), 2 * PEER_LC)
    return out.reshape(B, S, D)


def kernel(x, w_in, w_gla_gate, b_gla_gate, gla_norm_g, sb_norm_g, w_out, ln1_g, ln1_b,
           peer_w_query, peer_sub_keys, peer_u, peer_v, ln2_g, ln2_b):
    for l in range(DEPTH):
        x = _layer(x, w_in[l], w_gla_gate[l], b_gla_gate[l], gla_norm_g[l], sb_norm_g[l], w_out[l],
                   ln1_g[l], ln1_b[l], peer_w_query[l], peer_sub_keys[l], peer_u[l], peer_v[l],
                   ln2_g[l], ln2_b[l])
    return x
```
